```python
import jax, jax.numpy as jnp
from jax import lax
import numpy as np

D_MODEL = 2048
BATCH = 4
SEQ = 2048
DEPTH = 1

HEAD_DIM = 128
ATTN_WIDTH = D_MODEL // 2
N_ATTN_HEADS = ATTN_WIDTH // HEAD_DIM
CONV_WIDTH = D_MODEL - ATTN_WIDTH
CONV_GROUPS = CONV_WIDTH // HEAD_DIM
CONV_KERNEL = 31
ROT_DIM = HEAD_DIM // 4
ROPE_THETA = 500000.0
MOBA_BLOCK = 256
MOBA_TOPK = 3
Q_CHUNK = 32
D_FF = 4 * D_MODEL
FFN_CONV_KERNEL = 3
IN_COLS = 3 * ATTN_WIDTH + 2 * CONV_WIDTH
EPS = 1e-5

kernel_name = "hymba_moba_conformer_convffn"


def rms_norm(x, g):
    xf = x.astype(jnp.float32)
    y = xf * lax.rsqrt(jnp.mean(xf * xf, axis=-1, keepdims=True) + EPS)
    return (y * g.astype(jnp.float32)).astype(x.dtype)


def layer_norm(x, g, b):
    xf = x.astype(jnp.float32)
    mu = jnp.mean(xf, axis=-1, keepdims=True)
    xc = xf - mu
    var = jnp.mean(xc * xc, axis=-1, keepdims=True)
    y = xc * lax.rsqrt(var + EPS) * g.astype(jnp.float32) + b.astype(jnp.float32)
    return y.astype(x.dtype)


def causal_depthwise_conv(x, w, bias):
    kw = w.shape[0]
    out = lax.conv_general_dilated(
        x, w[:, None, :].astype(x.dtype), window_strides=(1,), padding=[(kw - 1, 0)],
        dimension_numbers=("NWC", "WIO", "NWC"), feature_group_count=x.shape[-1])
    return out + bias.astype(x.dtype)


def partial_rotary(x, pos):
    half = ROT_DIM // 2
    inv_freq = ROPE_THETA ** (-jnp.arange(half, dtype=jnp.float32) / half)
    ang = pos.astype(jnp.float32)[:, None] * inv_freq[None, :]
    cos = jnp.cos(ang).astype(x.dtype)
    sin = jnp.sin(ang).astype(x.dtype)
    x1 = x[..., :half]
    x2 = x[..., half:ROT_DIM]
    return jnp.concatenate([x1 * cos - x2 * sin, x2 * cos + x1 * sin, x[..., ROT_DIM:]], axis=-1)


def moba_attention(q, k, v):
    b, h, s, d = q.shape
    nb = -(-s // MOBA_BLOCK)
    pad = nb * MOBA_BLOCK - s
    kp = jnp.pad(k, ((0, 0), (0, 0), (0, pad), (0, 0)))
    vp = jnp.pad(v, ((0, 0), (0, 0), (0, pad), (0, 0)))
    kb = kp.reshape(b, h, nb, MOBA_BLOCK, d)
    vb = vp.reshape(b, h, nb, MOBA_BLOCK, d)
    kmean = jnp.mean(kb.astype(jnp.float32), axis=3).astype(q.dtype)
    n_sel = min(MOBA_TOPK, nb)
    scale = d ** -0.5
    bi = jnp.arange(b)[:, None, None, None]
    hi = jnp.arange(h)[None, :, None, None]
    blk_ids = jnp.arange(nb)

    def chunk(ci):
        start = ci * Q_CHUNK
        blk = start // MOBA_BLOCK
        qc = lax.dynamic_slice_in_dim(q, start, Q_CHUNK, axis=2)
        gate = jnp.einsum("bhcd,bhnd->bhcn", qc, kmean).astype(jnp.float32)
        gate = jnp.where(blk_ids < blk, gate, -jnp.inf)
        _, idx = lax.top_k(gate, n_sel)
        valid = idx < blk
        ksel = kb[bi, hi, idx]
        vsel = vb[bi, hi, idx]
        s_sel = jnp.einsum("bhcd,bhctkd->bhctk", qc, ksel).astype(jnp.float32) * scale
        s_sel = jnp.where(valid[..., None], s_sel, -jnp.inf)
        s_sel = s_sel.reshape(b, h, Q_CHUNK, n_sel * MOBA_BLOCK)
        kown = lax.dynamic_slice_in_dim(kp, blk * MOBA_BLOCK, MOBA_BLOCK, axis=2)
        vown = lax.dynamic_slice_in_dim(vp, blk * MOBA_BLOCK, MOBA_BLOCK, axis=2)
        s_own = jnp.einsum("bhcd,bhkd->bhck", qc, kown).astype(jnp.float32) * scale
        qpos = start + jnp.arange(Q_CHUNK)
        kpos = blk * MOBA_BLOCK + jnp.arange(MOBA_BLOCK)
        s_own = jnp.where(kpos[None, :] <= qpos[:, None], s_own, -jnp.inf)
        p = jax.nn.softmax(jnp.concatenate([s_sel, s_own], axis=-1), axis=-1).astype(q.dtype)
        p_sel = p[..., : n_sel * MOBA_BLOCK].reshape(b, h, Q_CHUNK, n_sel, MOBA_BLOCK)
        p_own = p[..., n_sel * MOBA_BLOCK:]
        return (jnp.einsum("bhctk,bhctkd->bhcd", p_sel, vsel)
                + jnp.einsum("bhck,bhkd->bhcd", p_own, vown))

    outs = lax.map(chunk, jnp.arange(s // Q_CHUNK))
    return outs.transpose(1, 2, 0, 3, 4).reshape(b, h, s, d)


def setup_inputs(seed: int = 0) -> dict:
    key = jax.random.key(seed)
    ks = jax.random.split(key, 16)
    f32 = jnp.float32
    nrm = lambda k, shape, s: jax.random.normal(k, shape, f32) * s
    return {
        "x": nrm(ks[0], (BATCH, SEQ, D_MODEL), 1.0),
        "norm_mix_g": 1.0 + nrm(ks[1], (DEPTH, D_MODEL), 0.02),
        "w_in": nrm(ks[2], (DEPTH, D_MODEL, IN_COLS), D_MODEL ** -0.5),
        "attn_out_g": 1.0 + nrm(ks[3], (DEPTH, ATTN_WIDTH), 0.02),
        "conv_dw_w": nrm(ks[4], (DEPTH, CONV_KERNEL, CONV_WIDTH), CONV_KERNEL ** -0.5),
        "conv_dw_b": nrm(ks[5], (DEPTH, CONV_WIDTH), 0.02),
        "conv_ln_g": 1.0 + nrm(ks[6], (DEPTH, CONV_WIDTH), 0.02),
        "conv_ln_b": nrm(ks[7], (DEPTH, CONV_WIDTH), 0.02),
        "w_out": nrm(ks[8], (DEPTH, D_MODEL, D_MODEL), D_MODEL ** -0.5),
        "norm_ffn_g": 1.0 + nrm(ks[9], (DEPTH, D_MODEL), 0.02),
        "w_up": nrm(ks[10], (DEPTH, D_MODEL, 2 * D_FF), D_MODEL ** -0.5),
        "ffn_dw_w": nrm(ks[11], (DEPTH, FFN_CONV_KERNEL, 2 * D_FF), FFN_CONV_KERNEL ** -0.5),
        "ffn_dw_b": nrm(ks[12], (DEPTH, 2 * D_FF), 0.02),
        "w_down": nrm(ks[13], (DEPTH, D_FF, D_MODEL), D_FF ** -0.5),
        "norm_final_g": 1.0 + nrm(ks[14], (D_MODEL,), 0.02),
    }


def reference(x, norm_mix_g, w_in, attn_out_g, conv_dw_w, conv_dw_b, conv_ln_g, conv_ln_b,
              w_out, norm_ffn_g, w_up, ffn_dw_w, ffn_dw_b, w_down, norm_final_g):
    b, s, _ = x.shape
    pos = jnp.arange(s, dtype=jnp.int32)
    for l in range(DEPTH):
        hn = rms_norm(x, norm_mix_g[l])
        u = jnp.einsum("bsd,de->bse", hn, w_in[l])
        q = u[..., :ATTN_WIDTH]
        k = u[..., ATTN_WIDTH:2 * ATTN_WIDTH]
        v = u[..., 2 * ATTN_WIDTH:3 * ATTN_WIDTH]
        c = u[..., 3 * ATTN_WIDTH:]

        to_heads = lambda t: t.reshape(b, s, N_ATTN_HEADS, HEAD_DIM).transpose(0, 2, 1, 3)
        qh = partial_rotary(to_heads(q), pos)
        kh = partial_rotary(to_heads(k), pos)
        vh = to_heads(v)
        a = moba_attention(qh, kh, vh).transpose(0, 2, 1, 3)
        a = rms_norm(a, attn_out_g[l].reshape(N_ATTN_HEADS, HEAD_DIM)).reshape(b, s, ATTN_WIDTH)

        cv, cg = c[..., :CONV_WIDTH], c[..., CONV_WIDTH:]
        g = cv * jax.nn.sigmoid(cg)
        g = causal_depthwise_conv(g, conv_dw_w[l], conv_dw_b[l])
        g = jax.nn.silu(layer_norm(g, conv_ln_g[l], conv_ln_b[l]))

        mixed = jnp.concatenate([a, g], axis=-1)
        x = x + jnp.einsum("bse,ed->bsd", mixed, w_out[l])

        hf = rms_norm(x, norm_ffn_g[l])
        up = jnp.einsum("bsd,df->bsf", hf, w_up[l])
        up = causal_depthwise_conv(up, ffn_dw_w[l], ffn_dw_b[l])
        act = jax.nn.silu(up[..., :D_FF]) * up[..., D_FF:]
        x = x + jnp.einsum("bsf,fd->bsd", act, w_down[l])
    return rms_norm(x, norm_final_g)
```

```python
import functools

import jax
import jax.numpy as jnp
from jax import lax
from jax.experimental import pallas as pl
from jax.experimental.pallas import tpu as pltpu

F32 = jnp.float32
BF16 = jnp.bfloat16

HEAD_DIM = 128
ROT_DIM = HEAD_DIM // 4
ROPE_THETA = 500000.0
MOBA_BLOCK = 256
MOBA_TOPK = 3
CONV_KERNEL = 31
FFN_CONV_KERNEL = 3
EPS = 1e-5

V7X_SUBLANES = 8
V7X_VMEM_BYTES = 64 * 1024 * 1024

CONV_HALO = 32
FFN_HALO = V7X_SUBLANES


def _vmem_limit(nbytes):
    return int(min(nbytes * 1.5 + (8 << 20), V7X_VMEM_BYTES - (4 << 20)))


def _rms(x, g):
    return x * lax.rsqrt(jnp.mean(x * x, axis=-1, keepdims=True) + EPS) * g


def _in_proj_kernel(x_ref, g_ref, w_ref, o_ref, hn_ref):
    @pl.when(pl.program_id(1) == 0)
    def _():
        hn_ref[...] = _rms(x_ref[...], g_ref[...]).astype(BF16)

    o_ref[...] = jnp.dot(hn_ref[...], w_ref[...], preferred_element_type=F32).astype(o_ref.dtype)


def _in_proj(x2d, g, w, tm=512, tn=1024):
    m, d = x2d.shape
    n = w.shape[1]
    est = 2 * tm * d * 4 + tm * d * 2 + 2 * d * tn * 2 + 2 * tm * tn * 2
    return pl.pallas_call(
        _in_proj_kernel,
        grid=(m // tm, n // tn),
        in_specs=[
            pl.BlockSpec((tm, d), lambda i, j: (i, 0)),
            pl.BlockSpec((1, d), lambda i, j: (0, 0)),
            pl.BlockSpec((d, tn), lambda i, j: (0, j)),
        ],
        out_specs=pl.BlockSpec((tm, tn), lambda i, j: (i, j)),
        out_shape=jax.ShapeDtypeStruct((m, n), BF16),
        scratch_shapes=[pltpu.VMEM((tm, d), BF16)],
        compiler_params=pltpu.CompilerParams(
            dimension_semantics=("arbitrary", "arbitrary"), vmem_limit_bytes=_vmem_limit(est)),
        name="in_proj",
    )(x2d, g.reshape(1, d), w)


def _rotate(x, cos, sin_lo, sin_hi):
    half = ROT_DIM // 2
    return (x * cos + pltpu.roll(x, HEAD_DIM - half, 1) * sin_lo + pltpu.roll(x, half, 1) * sin_hi)


def _attn_kernel(invf_ref, q_ref, k_ref, v_ref, g_ref, o_ref,
                 cos_ref, slo_ref, shi_ref, qr_ref, kr_ref, *, seq):
    nb = seq // MOBA_BLOCK
    half = ROT_DIM // 2

    @pl.when((pl.program_id(0) == 0) & (pl.program_id(1) == 0))
    def _():
        pos = lax.broadcasted_iota(jnp.int32, (seq, HEAD_DIM), 0).astype(F32)
        lane = lax.broadcasted_iota(jnp.int32, (seq, HEAD_DIM), 1)
        ang = pos * invf_ref[...]
        sin = jnp.sin(ang)
        cos_ref[...] = jnp.cos(ang)
        slo_ref[...] = jnp.where(lane < half, -sin, 0.0)
        shi_ref[...] = jnp.where(lane >= half, sin, 0.0)

    cos, slo, shi = cos_ref[...], slo_ref[...], shi_ref[...]
    qr_ref[...] = _rotate(q_ref[...].astype(F32), cos, slo, shi)
    kr_ref[...] = _rotate(k_ref[...].astype(F32), cos, slo, shi)

    kmean = jnp.mean(kr_ref[...].reshape(nb, MOBA_BLOCK, HEAD_DIM), axis=1)
    gate = lax.dot_general(qr_ref[...], kmean, (((1,), (1,)), ((), ())),
                           precision=lax.Precision.HIGHEST, preferred_element_type=F32)

    scale = HEAD_DIM ** -0.5
    row = lax.broadcasted_iota(jnp.int32, (MOBA_BLOCK, MOBA_BLOCK), 0)
    col = lax.broadcasted_iota(jnp.int32, (MOBA_BLOCK, MOBA_BLOCK), 1)
    causal = col <= row
    gain = g_ref[...]

    for i in range(nb):
        rows = slice(i * MOBA_BLOCK, (i + 1) * MOBA_BLOCK)
        klen = (i + 1) * MOBA_BLOCK
        qi = qr_ref[rows, :].astype(BF16)
        kk = kr_ref[0:klen, :].astype(BF16)
        s = lax.dot_general(qi, kk, (((1,), (1,)), ((), ())), preferred_element_type=F32) * scale

        keep = None
        if i > MOBA_TOPK:
            gi = gate[rows, :]
            cols_ = [gi[:, j:j + 1] for j in range(i)]
            keep = []
            for j in range(i):
                rank = jnp.zeros((MOBA_BLOCK, 1), jnp.int32)
                for jj in range(i):
                    if jj == j:
                        continue
                    beats = cols_[jj] > cols_[j]
                    if jj < j:
                        beats = beats | (cols_[jj] == cols_[j])
                    rank = rank + beats.astype(jnp.int32)
                keep.append(rank < MOBA_TOPK)

        pieces = []
        for j in range(i):
            sj = s[:, j * MOBA_BLOCK:(j + 1) * MOBA_BLOCK]
            if keep is not None:
                sj = jnp.where(keep[j], sj, -jnp.inf)
            pieces.append(sj)
        pieces.append(jnp.where(causal, s[:, i * MOBA_BLOCK:], -jnp.inf))

        m = functools.reduce(jnp.maximum, [jnp.max(p, axis=-1, keepdims=True) for p in pieces])
        ps = [jnp.exp(p - m) for p in pieces]
        denom = functools.reduce(jnp.add, [jnp.sum(p, axis=-1, keepdims=True) for p in ps])
        pcat = jnp.concatenate(ps, axis=1).astype(BF16) if len(ps) > 1 else ps[0].astype(BF16)
        o = jnp.dot(pcat, v_ref[0:klen, :], preferred_element_type=F32) / denom
        o_ref[rows, :] = _rms(o, gain).astype(o_ref.dtype)


def _moba_attn(u3, attn_g, n_heads):
    b, s, _ = u3.shape
    half = ROT_DIM // 2
    inv_freq = ROPE_THETA ** (-jnp.arange(half, dtype=F32) / half)
    invf = jnp.concatenate([inv_freq, inv_freq, jnp.zeros((HEAD_DIM - ROT_DIM,), F32)]).reshape(1, HEAD_DIM)
    blk = lambda off: pl.BlockSpec((None, s, HEAD_DIM), lambda bi, hi: (bi, 0, off + hi))
    est = 3 * 2 * s * HEAD_DIM * 2 + 2 * s * HEAD_DIM * 2 + 5 * s * HEAD_DIM * 4 + 8 * MOBA_BLOCK * s * 4
    return pl.pallas_call(
        functools.partial(_attn_kernel, seq=s),
        grid=(b, n_heads),
        in_specs=[
            pl.BlockSpec((1, HEAD_DIM), lambda bi, hi: (0, 0)),
            blk(0), blk(n_heads), blk(2 * n_heads),
            pl.BlockSpec((1, HEAD_DIM), lambda bi, hi: (0, hi)),
        ],
        out_specs=pl.BlockSpec((None, s, HEAD_DIM), lambda bi, hi: (bi, 0, hi)),
        out_shape=jax.ShapeDtypeStruct((b, s, n_heads * HEAD_DIM), BF16),
        scratch_shapes=[pltpu.VMEM((s, HEAD_DIM), F32)] * 5,
        compiler_params=pltpu.CompilerParams(
            dimension_semantics=("arbitrary", "arbitrary"), vmem_limit_bytes=_vmem_limit(est)),
        name="moba_attn",
    )(invf, u3, u3, u3, attn_g.reshape(1, n_heads * HEAD_DIM))


def _conv_kernel(cv_ref, cg_ref, w_ref, b_ref, lg_ref, lb_ref, o_ref, gp_ref, *, t_rows, chunk):
    @pl.when(pl.program_id(1) == 0)
    def _():
        gp_ref[0:CONV_HALO, :] = jnp.zeros((CONV_HALO, gp_ref.shape[1]), F32)

    @pl.when(pl.program_id(1) > 0)
    def _():
        gp_ref[0:CONV_HALO, :] = gp_ref[t_rows:t_rows + CONV_HALO, :]

    gp_ref[CONV_HALO:, :] = cv_ref[...].astype(F32) * jax.nn.sigmoid(cg_ref[...].astype(F32))

    bias, lg, lb = b_ref[...], lg_ref[...], lb_ref[...]
    for c in range(t_rows // chunk):
        base = CONV_HALO + c * chunk - (CONV_KERNEL - 1)
        acc = jnp.broadcast_to(bias, (chunk, bias.shape[1]))
        for k in range(CONV_KERNEL):
            acc = acc + gp_ref[base + k:base + k + chunk, :] * w_ref[k:k + 1, :]
        mu = jnp.mean(acc, axis=-1, keepdims=True)
        xc = acc - mu
        var = jnp.mean(xc * xc, axis=-1, keepdims=True)
        y = xc * lax.rsqrt(var + EPS) * lg + lb
        o_ref[c * chunk:(c + 1) * chunk, :] = (y * jax.nn.sigmoid(y)).astype(o_ref.dtype)


def _conv_mod(u3, col_off, width, w, bias, ln_g, ln_b, t_rows=256, chunk=32):
    b, s, _ = u3.shape
    cblk = col_off // width
    tile_f32 = t_rows * width * 4
    est = (2 * 2 * t_rows * width * 2 + 2 * t_rows * width * 2 + (t_rows + CONV_HALO) * width * 4
           + 40 * width * 4 + 12 * tile_f32)
    vec = lambda a: a.reshape(1, width)
    vspec = pl.BlockSpec((1, width), lambda bi, ti: (0, 0))
    return pl.pallas_call(
        functools.partial(_conv_kernel, t_rows=t_rows, chunk=chunk),
        grid=(b, s // t_rows),
        in_specs=[
            pl.BlockSpec((None, t_rows, width), lambda bi, ti: (bi, ti, cblk)),
            pl.BlockSpec((None, t_rows, width), lambda bi, ti: (bi, ti, cblk + 1)),
            pl.BlockSpec((CONV_KERNEL, width), lambda bi, ti: (0, 0)),
            vspec, vspec, vspec,
        ],
        out_specs=pl.BlockSpec((None, t_rows, width), lambda bi, ti: (bi, ti, 0)),
        out_shape=jax.ShapeDtypeStruct((b, s, width), BF16),
        scratch_shapes=[pltpu.VMEM((t_rows + CONV_HALO, width), F32)],
        compiler_params=pltpu.CompilerParams(
            dimension_semantics=("arbitrary", "arbitrary"), vmem_limit_bytes=_vmem_limit(est)),
        name="conv_mod",
    )(u3, u3, w, vec(bias), vec(ln_g), vec(ln_b))


def _out_proj_kernel(x_ref, a_ref, c_ref, wa_ref, wc_ref, g_ref, x1_ref, hf_ref):
    y = (jnp.dot(a_ref[...], wa_ref[...], preferred_element_type=F32)
         + jnp.dot(c_ref[...], wc_ref[...], preferred_element_type=F32))
    x1 = x_ref[...] + y
    x1_ref[...] = x1
    hf_ref[...] = _rms(x1, g_ref[...]).astype(hf_ref.dtype)


def _out_proj(x2d, a2d, c2d, w_out, g, tm=256):
    m, d = x2d.shape
    ka, kc = a2d.shape[1], c2d.shape[1]
    assert ka == kc and ka + kc == w_out.shape[0]
    est = 2 * tm * d * 4 * 2 + 2 * tm * d * 2 + 2 * 2 * tm * ka * 2 + 2 * 2 * ka * d * 2
    return pl.pallas_call(
        _out_proj_kernel,
        grid=(m // tm,),
        in_specs=[
            pl.BlockSpec((tm, d), lambda i: (i, 0)),
            pl.BlockSpec((tm, ka), lambda i: (i, 0)),
            pl.BlockSpec((tm, kc), lambda i: (i, 0)),
            pl.BlockSpec((ka, d), lambda i: (0, 0)),
            pl.BlockSpec((kc, d), lambda i: (1, 0)),
            pl.BlockSpec((1, d), lambda i: (0, 0)),
        ],
        out_specs=[pl.BlockSpec((tm, d), lambda i: (i, 0)), pl.BlockSpec((tm, d), lambda i: (i, 0))],
        out_shape=[jax.ShapeDtypeStruct((m, d), F32), jax.ShapeDtypeStruct((m, d), BF16)],
        compiler_params=pltpu.CompilerParams(
            dimension_semantics=("arbitrary",), vmem_limit_bytes=_vmem_limit(est)),
        name="out_proj",
    )(x2d, a2d, c2d, w_out, w_out, g.reshape(1, d))


def _ffn_kernel(hf_ref, x1_ref, wua_ref, wub_ref, cwa_ref, cwb_ref, cba_ref, cbb_ref, wd_ref, gf_ref,
                o_ref, acc_ref, sa_ref, sb_ref, carry_a_ref, carry_b_ref,
                *, tm, tiles_per_seq, final_norm):
    i, f = pl.program_id(0), pl.program_id(1)
    seq_start = (i % tiles_per_seq) == 0
    hf = hf_ref[...]

    def conv_half(wu_ref, cw_ref, cb_ref, s_ref, carry_ref):
        @pl.when(seq_start)
        def _():
            s_ref[0:FFN_HALO, :] = jnp.zeros((FFN_HALO, s_ref.shape[1]), F32)

        @pl.when(jnp.logical_not(seq_start))
        def _():
            s_ref[0:FFN_HALO, :] = carry_ref[f]

        s_ref[FFN_HALO:, :] = jnp.dot(hf, wu_ref[...], preferred_element_type=F32)
        carry_ref[f] = s_ref[tm:tm + FFN_HALO, :]
        y = cb_ref[...]
        for k in range(FFN_CONV_KERNEL):
            off = FFN_HALO - (FFN_CONV_KERNEL - 1) + k
            y = y + s_ref[off:off + tm, :] * cw_ref[k:k + 1, :]
        return y

    ya = conv_half(wua_ref, cwa_ref, cba_ref, sa_ref, carry_a_ref)
    yb = conv_half(wub_ref, cwb_ref, cbb_ref, sb_ref, carry_b_ref)
    act = (ya * jax.nn.sigmoid(ya) * yb).astype(BF16)
    part = jnp.dot(act, wd_ref[...], preferred_element_type=F32)

    @pl.when(f == 0)
    def _():
        acc_ref[...] = part

    @pl.when(f > 0)
    def _():
        acc_ref[...] += part

    @pl.when(f == pl.num_programs(1) - 1)
    def _():
        x2 = x1_ref[...] + acc_ref[...]
        o_ref[...] = _rms(x2, gf_ref[...]) if final_norm else x2


def _conv_ffn(hf, x1, w_up, cw, cb, w_down, g_final, seq, final_norm, tm=512, tf=512):
    m, d = hf.shape
    dff = w_down.shape[0]
    nf = dff // tf
    est = (2 * tm * d * 2 + 2 * tm * d * 4 * 2 + tm * d * 4 + 2 * 2 * d * tf * 2 + 2 * tf * d * 2
           + 2 * (tm + FFN_HALO) * tf * 4 + 2 * nf * FFN_HALO * tf * 4 + 6 * tm * tf * 4)
    vspec_a = lambda r: pl.BlockSpec((r, tf), lambda i, f: (0, f))
    vspec_b = lambda r: pl.BlockSpec((r, tf), lambda i, f: (0, nf + f))
    cb2 = cb.reshape(1, 2 * dff)
    return pl.pallas_call(
        functools.partial(_ffn_kernel, tm=tm, tiles_per_seq=seq // tm, final_norm=final_norm),
        grid=(m // tm, nf),
        in_specs=[
            pl.BlockSpec((tm, d), lambda i, f: (i, 0)),
            pl.BlockSpec((tm, d), lambda i, f: (i, 0)),
            pl.BlockSpec((d, tf), lambda i, f: (0, f)),
            pl.BlockSpec((d, tf), lambda i, f: (0, nf + f)),
            vspec_a(FFN_CONV_KERNEL), vspec_b(FFN_CONV_KERNEL), vspec_a(1), vspec_b(1),
            pl.BlockSpec((tf, d), lambda i, f: (f, 0)),
            pl.BlockSpec((1, d), lambda i, f: (0, 0)),
        ],
        out_specs=pl.BlockSpec((tm, d), lambda i, f: (i, 0)),
        out_shape=jax.ShapeDtypeStruct((m, d), F32),
        scratch_shapes=[
            pltpu.VMEM((tm, d), F32),
            pltpu.VMEM((tm + FFN_HALO, tf), F32),
            pltpu.VMEM((tm + FFN_HALO, tf), F32),
            pltpu.VMEM((nf, FFN_HALO, tf), F32),
            pltpu.VMEM((nf, FFN_HALO, tf), F32),
        ],
        compiler_params=pltpu.CompilerParams(
            dimension_semantics=("arbitrary", "arbitrary"), vmem_limit_bytes=_vmem_limit(est)),
        name="conv_ffn",
    )(hf, x1, w_up, w_up, cw, cw, cb2, cb2, w_down, g_final.reshape(1, d))


def kernel(x, norm_mix_g, w_in, attn_out_g, conv_dw_w, conv_dw_b, conv_ln_g, conv_ln_b,
           w_out, norm_ffn_g, w_up, ffn_dw_w, ffn_dw_b, w_down, norm_final_g):
    b, s, d = x.shape
    depth = w_in.shape[0]
    attn_width = attn_out_g.shape[1]
    conv_width = conv_dw_b.shape[1]
    n_heads = attn_width // HEAD_DIM
    assert s % MOBA_BLOCK == 0 and w_in.shape[2] == 3 * attn_width + 2 * conv_width

    xf = x.reshape(b * s, d)
    for l in range(depth):
        u = _in_proj(xf, norm_mix_g[l], w_in[l].astype(BF16))
        u3 = u.reshape(b, s, -1)
        a = _moba_attn(u3, attn_out_g[l], n_heads)
        c = _conv_mod(u3, 3 * attn_width, conv_width, conv_dw_w[l], conv_dw_b[l], conv_ln_g[l], conv_ln_b[l])
        x1, hf = _out_proj(xf, a.reshape(b * s, attn_width), c.reshape(b * s, conv_width),
                           w_out[l].astype(BF16), norm_ffn_g[l])
        xf = _conv_ffn(hf, x1, w_up[l].astype(BF16), ffn_dw_w[l], ffn_dw_b[l], w_down[l].astype(BF16),
                       norm_final_g, s, final_norm=(l == depth - 1))
    return xf.reshape(b, s, d)
```

```python
import functools

import jax
import jax.numpy as jnp
from jax import lax
from jax.experimental import pallas as pl
from jax.experimental.pallas import tpu as pltpu

F32 = jnp.float32
BF16 = jnp.bfloat16

HEAD_DIM = 128
ROT_DIM = HEAD_DIM // 4
ROPE_THETA = 500000.0
MOBA_BLOCK = 256
MOBA_TOPK = 3
CONV_KERNEL = 31
FFN_CONV_KERNEL = 3
EPS = 1e-5

V7X_SUBLANES = 8
V7X_VMEM_BYTES = 64 * 1024 * 1024

CONV_HALO = 32
FFN_HALO = V7X_SUBLANES


def _vmem_limit(nbytes):
    return int(min(nbytes * 1.5 + (8 << 20), V7X_VMEM_BYTES - (4 << 20)))


def _rms(x, g):
    return x * lax.rsqrt(jnp.mean(x * x, axis=-1, keepdims=True) + EPS) * g


def _in_proj_kernel(x_ref, g_ref, w_ref, o_ref, hn_ref):
    @pl.when(pl.program_id(1) == 0)
    def _():
        hn_ref[...] = _rms(x_ref[...], g_ref[...]).astype(BF16)

    o_ref[...] = jnp.dot(hn_ref[...], w_ref[...], preferred_element_type=F32).astype(o_ref.dtype)


def _in_proj(x2d, g, w, tm=512, tn=1024):
    m, d = x2d.shape
    n = w.shape[1]
    est = 2 * tm * d * 4 + tm * d * 2 + 2 * d * tn * 2 + 2 * tm * tn * 2
    return pl.pallas_call(
        _in_proj_kernel,
        grid=(m // tm, n // tn),
        in_specs=[
            pl.BlockSpec((tm, d), lambda i, j: (i, 0)),
            pl.BlockSpec((1, d), lambda i, j: (0, 0)),
            pl.BlockSpec((d, tn), lambda i, j: (0, j)),
        ],
        out_specs=pl.BlockSpec((tm, tn), lambda i, j: (i, j)),
        out_shape=jax.ShapeDtypeStruct((m, n), BF16),
        scratch_shapes=[pltpu.VMEM((tm, d), BF16)],
        compiler_params=pltpu.CompilerParams(
            dimension_semantics=("arbitrary", "arbitrary"), vmem_limit_bytes=_vmem_limit(est)),
        name="in_proj",
    )(x2d, g.reshape(1, d), w)


def _rotate(x, cos, sin_lo, sin_hi):
    half = ROT_DIM // 2
    return (x * cos + pltpu.roll(x, HEAD_DIM - half, 1) * sin_lo + pltpu.roll(x, half, 1) * sin_hi)


def _attn_kernel(invf_ref, q_ref, k_ref, v_ref, g_ref, o_ref,
                 cos_ref, slo_ref, shi_ref, qr_ref, kr_ref, *, seq):
    nb = seq // MOBA_BLOCK
    half = ROT_DIM // 2

    @pl.when((pl.program_id(0) == 0) & (pl.program_id(1) == 0))
    def _():
        pos = lax.broadcasted_iota(jnp.int32, (seq, HEAD_DIM), 0).astype(F32)
        lane = lax.broadcasted_iota(jnp.int32, (seq, HEAD_DIM), 1)
        ang = pos * invf_ref[...]
        sin = jnp.sin(ang)
        cos_ref[...] = jnp.cos(ang)
        slo_ref[...] = jnp.where(lane < half, -sin, 0.0)
        shi_ref[...] = jnp.where(lane >= half, sin, 0.0)

    cos, slo, shi = cos_ref[...], slo_ref[...], shi_ref[...]
    qr_ref[...] = _rotate(q_ref[...].astype(F32), cos, slo, shi)
    kr_ref[...] = _rotate(k_ref[...].astype(F32), cos, slo, shi)

    kmean = jnp.mean(kr_ref[...].reshape(nb, MOBA_BLOCK, HEAD_DIM), axis=1)
    gate = lax.dot_general(qr_ref[...], kmean, (((1,), (1,)), ((), ())),
                           precision=lax.Precision.HIGHEST, preferred_element_type=F32)

    scale = HEAD_DIM ** -0.5
    row = lax.broadcasted_iota(jnp.int32, (MOBA_BLOCK, MOBA_BLOCK), 0)
    col = lax.broadcasted_iota(jnp.int32, (MOBA_BLOCK, MOBA_BLOCK), 1)
    causal = col <= row
    gain = g_ref[...]

    for i in range(nb):
        rows = slice(i * MOBA_BLOCK, (i + 1) * MOBA_BLOCK)
        klen = (i + 1) * MOBA_BLOCK
        qi = qr_ref[rows, :].astype(BF16)
        kk = kr_ref[0:klen, :].astype(BF16)
        s = lax.dot_general(qi, kk, (((1,), (1,)), ((), ())), preferred_element_type=F32) * scale

        keep = None
        if i > MOBA_TOPK:
            gi = gate[rows, :]
            cols_ = [gi[:, j:j + 1] for j in range(i)]
            keep = []
            for j in range(i):
                rank = jnp.zeros((MOBA_BLOCK, 1), jnp.int32)
                for jj in range(i):
                    if jj == j:
                        continue
                    beats = cols_[jj] > cols_[j]
                    if jj < j:
                        beats = beats | (cols_[jj] == cols_[j])
                    rank = rank + beats.astype(jnp.int32)
                keep.append(rank < MOBA_TOPK)

        pieces = []
        for j in range(i):
            sj = s[:, j * MOBA_BLOCK:(j + 1) * MOBA_BLOCK]
            if keep is not None:
                sj = jnp.where(keep[j], sj, -jnp.inf)
            pieces.append(sj)
        pieces.append(jnp.where(causal, s[:, i * MOBA_BLOCK:], -jnp.inf))

        m = functools.reduce(jnp.maximum, [jnp.max(p, axis=-1, keepdims=True) for p in pieces])
        ps = [jnp.exp(p - m) for p in pieces]
        denom = functools.reduce(jnp.add, [jnp.sum(p, axis=-1, keepdims=True) for p in ps])
        pcat = jnp.concatenate(ps, axis=1).astype(BF16) if len(ps) > 1 else ps[0].astype(BF16)
        o = jnp.dot(pcat, v_ref[0:klen, :], preferred_element_type=F32) / denom
        o_ref[rows, :] = _rms(o, gain).astype(o_ref.dtype)


def _moba_attn(u3, attn_g, n_heads):
    b, s, _ = u3.shape
    half = ROT_DIM // 2
    inv_freq = ROPE_THETA ** (-jnp.arange(half, dtype=F32) / half)
    invf = jnp.concatenate([inv_freq, inv_freq, jnp.zeros((HEAD_DIM - ROT_DIM,), F32)]).reshape(1, HEAD_DIM)
    blk = lambda off: pl.BlockSpec((None, s, HEAD_DIM), lambda bi, hi: (bi, 0, off + hi))
    est = 3 * 2 * s * HEAD_DIM * 2 + 2 * s * HEAD_DIM * 2 + 5 * s * HEAD_DIM * 4 + 8 * MOBA_BLOCK * s * 4
    return pl.pallas_call(
        functools.partial(_attn_kernel, seq=s),
        grid=(b, n_heads),
        in_specs=[
            pl.BlockSpec((1, HEAD_DIM), lambda bi, hi: (0, 0)),
            blk(0), blk(n_heads), blk(2 * n_heads),
            pl.BlockSpec((1, HEAD_DIM), lambda bi, hi: (0, hi)),
        ],
        out_specs=pl.BlockSpec((None, s, HEAD_DIM), lambda bi, hi: (bi, 0, hi)),
        out_shape=jax.ShapeDtypeStruct((b, s, n_heads * HEAD_DIM), BF16),
        scratch_shapes=[pltpu.VMEM((s, HEAD_DIM), F32)] * 5,
        compiler_params=pltpu.CompilerParams(
            dimension_semantics=("arbitrary", "arbitrary"), vmem_limit_bytes=_vmem_limit(est)),
        name="moba_attn",
    )(invf, u3, u3, u3, attn_g.reshape(1, n_heads * HEAD_DIM))


def _conv_kernel(cv_ref, cg_ref, w_ref, b_ref, lg_ref, lb_ref, o_ref, gp_ref, *, t_rows, chunk):
    @pl.when(pl.program_id(1) == 0)
    def _():
        gp_ref[0:CONV_HALO, :] = jnp.zeros((CONV_HALO, gp_ref.shape[1]), F32)

    @pl.when(pl.program_id(1) > 0)
    def _():
        gp_ref[0:CONV_HALO, :] = gp_ref[t_rows:t_rows + CONV_HALO, :]

    gp_ref[CONV_HALO:, :] = cv_ref[...].astype(F32) * jax.nn.sigmoid(cg_ref[...].astype(F32))

    bias, lg, lb = b_ref[...], lg_ref[...], lb_ref[...]
    for c in range(t_rows // chunk):
        base = CONV_HALO + c * chunk - (CONV_KERNEL - 1)
        acc = jnp.broadcast_to(bias, (chunk, bias.shape[1]))
        for k in range(CONV_KERNEL):
            acc = acc + gp_ref[base + k:base + k + chunk, :] * w_ref[k:k + 1, :]
        mu = jnp.mean(acc, axis=-1, keepdims=True)
        xc = acc - mu
        var = jnp.mean(xc * xc, axis=-1, keepdims=True)
        y = xc * lax.rsqrt(var + EPS) * lg + lb
        o_ref[c * chunk:(c + 1) * chunk, :] = (y * jax.nn.sigmoid(y)).astype(o_ref.dtype)


def _conv_mod(u3, col_off, width, w, bias, ln_g, ln_b, t_rows=256, chunk=32):
    b, s, _ = u3.shape
    cblk = col_off // width
    tile_f32 = t_rows * width * 4
    est = (2 * 2 * t_rows * width * 2 + 2 * t_rows * width * 2 + (t_rows + CONV_HALO) * width * 4
           + 40 * width * 4 + 12 * tile_f32)
    vec = lambda a: a.reshape(1, width)
    vspec = pl.BlockSpec((1, width), lambda bi, ti: (0, 0))
    return pl.pallas_call(
        functools.partial(_conv_kernel, t_rows=t_rows, chunk=chunk),
        grid=(b, s // t_rows),
        in_specs=[
            pl.BlockSpec((None, t_rows, width), lambda bi, ti: (bi, ti, cblk)),
            pl.BlockSpec((None, t_rows, width), lambda bi, ti: (bi, ti, cblk + 1)),
            pl.BlockSpec((CONV_KERNEL, width), lambda bi, ti: (0, 0)),
            vspec, vspec, vspec,
        ],
        out_specs=pl.BlockSpec((None, t_rows, width), lambda bi, ti: (bi, ti, 0)),
        out_shape=jax.ShapeDtypeStruct((b, s, width), BF16),
        scratch_shapes=[pltpu.VMEM((t_rows + CONV_HALO, width), F32)],
        compiler_params=pltpu.CompilerParams(
            dimension_semantics=("arbitrary", "arbitrary"), vmem_limit_bytes=_vmem_limit(est)),
        name="conv_mod",
    )(u3, u3, w, vec(bias), vec(ln_g), vec(ln_b))


def _out_proj_kernel(x_ref, a_ref, c_ref, wa_ref, wc_ref, g_ref, x1_ref, hf_ref):
    y = (jnp.dot(a_ref[...], wa_ref[...], preferred_element_type=F32)
         + jnp.dot(c_ref[...], wc_ref[...], preferred_element_type=F32))
    x1 = x_ref[...] + y
    x1_ref[...] = x1
    hf_ref[...] = _rms(x1, g_ref[...]).astype(hf_ref.dtype)


def _out_proj(x2d, a2d, c2d, w_out, g, tm=256):
    m, d = x2d.shape
    ka, kc = a2d.shape[1], c2d.shape[1]
    assert ka == kc and ka + kc == w_out.shape[0]
    est = 2 * tm * d * 4 * 2 + 2 * tm * d * 2 + 2 * 2 * tm * ka * 2 + 2 * 2 * ka * d * 2
    return pl.pallas_call(
        _out_proj_kernel,
        grid=(m // tm,),
        in_specs=[
            pl.BlockSpec((tm, d), lambda i: (i, 0)),
            pl.BlockSpec((tm, ka), lambda i: (i, 0)),
            pl.BlockSpec((tm, kc), lambda i: (i, 0)),
            pl.BlockSpec((ka, d), lambda i: (0, 0)),
            pl.BlockSpec((kc, d), lambda i: (1, 0)),
            pl.BlockSpec((1, d), lambda i: (0, 0)),
        ],
        out_specs=[pl.BlockSpec((tm, d), lambda i: (i, 0)), pl.BlockSpec((tm, d), lambda i: (i, 0))],
        out_shape=[jax.ShapeDtypeStruct((m, d), F32), jax.ShapeDtypeStruct((m, d), BF16)],
        compiler_params=pltpu.CompilerParams(
            dimension_semantics=("arbitrary",), vmem_limit_bytes=_vmem_limit(est)),
        name="out_proj",
    )(x2d, a2d, c2d, w_out, w_out, g.reshape(1, d))


FFN_PIPE_DEPTH = 1
FFN_ROW_CHUNK = 64
FFN_LANE_CHUNK = 128
FFN_COL_CHUNK = 512


def _ffn_kernel(hf_ref, x1_ref, wua_ref, wub_ref, cwa_ref, cwb_ref, cba_ref, cbb_ref, wd_ref, gf_ref,
                o_ref, acc_ref, sa0_ref, sb0_ref, sa1_ref, sb1_ref, act_ref,
                carry_a_ref, carry_b_ref, *, tm, nf, n_items, tiles_per_seq, final_norm):
    t = pl.program_id(0)
    ta = jnp.minimum(t, n_items - 1)
    fa = ta % nf
    seq_start = ((ta // nf) % tiles_per_seq) == 0
    d = acc_ref.shape[1]

    @pl.when(t == 0)
    def _():
        for ref in (acc_ref, sa1_ref, sb1_ref, carry_a_ref, carry_b_ref):
            ref[...] = jnp.zeros(ref.shape, ref.dtype)

    def body(s_w, s_r):
        hf = hf_ref[...]

        def up_half(wu_ref, sw_ref, carry_ref):
            sw_ref[0:FFN_HALO, :] = jnp.where(seq_start, 0.0, carry_ref[fa])
            sw_ref[FFN_HALO:, :] = jnp.dot(hf, wu_ref[...], preferred_element_type=F32)
            carry_ref[fa] = sw_ref[tm:tm + FFN_HALO, :]

        up_half(wua_ref, s_w[0], carry_a_ref)

        def taps(cw_ref, cb_ref, cols):
            bc = lambda v: jnp.broadcast_to(v, (FFN_ROW_CHUNK, FFN_LANE_CHUNK))
            return [bc(cw_ref[k:k + 1, cols]) for k in range(FFN_CONV_KERNEL)], bc(cb_ref[:, cols])

        def conv(sr_ref, w, bias, r0, cols):
            y = bias
            for k in range(FFN_CONV_KERNEL):
                off = r0 + FFN_HALO - (FFN_CONV_KERNEL - 1) + k
                y = y + sr_ref[off:off + FFN_ROW_CHUNK, cols] * w[k]
            return y

        for c0 in range(0, act_ref.shape[1], FFN_LANE_CHUNK):
            cols = slice(c0, c0 + FFN_LANE_CHUNK)
            wa, ba = taps(cwa_ref, cba_ref, cols)
            wb, bb = taps(cwb_ref, cbb_ref, cols)
            for r0 in range(0, tm, FFN_ROW_CHUNK):
                ya = conv(s_r[0], wa, ba, r0, cols)
                yb = conv(s_r[1], wb, bb, r0, cols)
                act_ref[r0:r0 + FFN_ROW_CHUNK, cols] = (ya * jax.nn.sigmoid(ya) * yb).astype(BF16)

        act = act_ref[...]
        for c0 in range(0, d, FFN_COL_CHUNK):
            acc_ref[:, c0:c0 + FFN_COL_CHUNK] += jnp.dot(
                act, wd_ref[:, c0:c0 + FFN_COL_CHUNK], preferred_element_type=F32)

        up_half(wub_ref, s_w[1], carry_b_ref)

    @pl.when(t % 2 == 0)
    def _():
        body((sa0_ref, sb0_ref), (sa1_ref, sb1_ref))

    @pl.when(t % 2 == 1)
    def _():
        body((sa1_ref, sb1_ref), (sa0_ref, sb0_ref))

    @pl.when(t == 0)
    def _():
        acc_ref[...] = jnp.zeros(acc_ref.shape, F32)

    @pl.when((t >= FFN_PIPE_DEPTH) & ((t - FFN_PIPE_DEPTH) % nf == nf - 1))
    def _():
        x2 = x1_ref[...] + acc_ref[...]
        o_ref[...] = _rms(x2, gf_ref[...]) if final_norm else x2
        acc_ref[...] = jnp.zeros(acc_ref.shape, F32)


def _conv_ffn(hf, x1, w_up, cw, cb, w_down, g_final, seq, final_norm, tm=512, tf=512):
    m, d = hf.shape
    dff = w_down.shape[0]
    nf = dff // tf
    n_items = (m // tm) * nf
    s_bytes = (tm + FFN_HALO) * tf * 4
    est = (2 * tm * d * 2 + 2 * tm * d * 4 * 2 + tm * d * 4 + 2 * 2 * d * tf * 2 + 2 * tf * d * 2
           + 4 * s_bytes + tm * tf * 2 + 2 * nf * FFN_HALO * tf * 4 + 4 * tm * tf * 4)

    item = lambda t, lag: jnp.clip(t - lag, 0, n_items - 1)
    row_a = lambda t: (item(t, 0) // nf, 0)
    row_c = lambda t: (item(t, FFN_PIPE_DEPTH) // nf, 0)
    vspec = lambda r, half: pl.BlockSpec((r, tf), lambda t: (0, half * nf + item(t, FFN_PIPE_DEPTH) % nf))
    cb2 = cb.reshape(1, 2 * dff)
    s_scratch = pltpu.VMEM((tm + FFN_HALO, tf), F32)
    return pl.pallas_call(
        functools.partial(_ffn_kernel, tm=tm, nf=nf, n_items=n_items, tiles_per_seq=seq // tm,
                          final_norm=final_norm),
        grid=(n_items + FFN_PIPE_DEPTH,),
        in_specs=[
            pl.BlockSpec((tm, d), row_a),
            pl.BlockSpec((tm, d), row_c),
            pl.BlockSpec((d, tf), lambda t: (0, item(t, 0) % nf)),
            pl.BlockSpec((d, tf), lambda t: (0, nf + item(t, 0) % nf)),
            vspec(FFN_CONV_KERNEL, 0), vspec(FFN_CONV_KERNEL, 1), vspec(1, 0), vspec(1, 1),
            pl.BlockSpec((tf, d), lambda t: (item(t, FFN_PIPE_DEPTH) % nf, 0)),
            pl.BlockSpec((1, d), lambda t: (0, 0)),
        ],
        out_specs=pl.BlockSpec((tm, d), row_c),
        out_shape=jax.ShapeDtypeStruct((m, d), F32),
        scratch_shapes=[
            pltpu.VMEM((tm, d), F32),
            s_scratch, s_scratch, s_scratch, s_scratch,
            pltpu.VMEM((tm, tf), BF16),
            pltpu.VMEM((nf, FFN_HALO, tf), F32),
            pltpu.VMEM((nf, FFN_HALO, tf), F32),
        ],
        compiler_params=pltpu.CompilerParams(
            dimension_semantics=("arbitrary",), vmem_limit_bytes=_vmem_limit(est)),
        name="conv_ffn",
    )(hf, x1, w_up, w_up, cw, cw, cb2, cb2, w_down, g_final.reshape(1, d))


def kernel(x, norm_mix_g, w_in, attn_out_g, conv_dw_w, conv_dw_b, conv_ln_g, conv_ln_b,
           w_out, norm_ffn_g, w_up, ffn_dw_w, ffn_dw_b, w_down, norm_final_g):
    b, s, d = x.shape
    depth = w_in.shape[0]
    attn_width = attn_out_g.shape[1]
    conv_width = conv_dw_b.shape[1]
    n_heads = attn_width // HEAD_DIM
    assert s % MOBA_BLOCK == 0 and w_in.shape[2] == 3 * attn_width + 2 * conv_width

    xf = x.reshape(b * s, d)
    for l in range(depth):
        u = _in_proj(xf, norm_mix_g[l], w_in[l].astype(BF16))
        u3 = u.reshape(b, s, -1)
        a = _moba_attn(u3, attn_out_g[l], n_heads)
        c = _conv_mod(u3, 3 * attn_width, conv_width, conv_dw_w[l], conv_dw_b[l], conv_ln_g[l], conv_ln_b[l])
        x1, hf = _out_proj(xf, a.reshape(b * s, attn_width), c.reshape(b * s, conv_width),
                           w_out[l].astype(BF16), norm_ffn_g[l])
        xf = _conv_ffn(hf, x1, w_up[l].astype(BF16), ffn_dw_w[l], ffn_dw_b[l], w_down[l].astype(BF16),
                       norm_final_g, s, final_norm=(l == depth - 1))
    return xf.reshape(b, s, d)
```

```python
import functools

import jax
import jax.numpy as jnp
from jax import lax
from jax.experimental import pallas as pl
from jax.experimental.pallas import tpu as pltpu

F32 = jnp.float32
BF16 = jnp.bfloat16

HEAD_DIM = 128
ROT_DIM = HEAD_DIM // 4
ROPE_THETA = 500000.0
MOBA_BLOCK = 256
MOBA_TOPK = 3
CONV_KERNEL = 31
FFN_CONV_KERNEL = 3
EPS = 1e-5

V7X_SUBLANES = 8
V7X_VMEM_BYTES = 64 * 1024 * 1024

CONV_HALO = 32
FFN_HALO = V7X_SUBLANES


def _vmem_limit(nbytes):
    return int(min(nbytes * 1.5 + (8 << 20), V7X_VMEM_BYTES - (4 << 20)))


def _rms(x, g):
    return x * lax.rsqrt(jnp.mean(x * x, axis=-1, keepdims=True) + EPS) * g


def _in_proj_kernel(x_ref, g_ref, w_ref, o_ref, hn_ref):
    @pl.when(pl.program_id(1) == 0)
    def _():
        hn_ref[...] = _rms(x_ref[...], g_ref[...]).astype(BF16)

    o_ref[...] = jnp.dot(hn_ref[...], w_ref[...], preferred_element_type=F32).astype(o_ref.dtype)


def _col_tiles(w, tn):
    k, n = w.shape
    return w.astype(BF16).reshape(k, n // tn, tn).transpose(1, 0, 2)


def _in_proj(x2d, g, w, tm=512, tn=1024):
    m, d = x2d.shape
    n = w.shape[1]
    est = 2 * tm * d * 4 + tm * d * 2 + 2 * d * tn * 2 + 2 * tm * tn * 2
    return pl.pallas_call(
        _in_proj_kernel,
        grid=(m // tm, n // tn),
        in_specs=[
            pl.BlockSpec((tm, d), lambda i, j: (i, 0)),
            pl.BlockSpec((1, d), lambda i, j: (0, 0)),
            pl.BlockSpec((None, d, tn), lambda i, j: (j, 0, 0)),
        ],
        out_specs=pl.BlockSpec((tm, tn), lambda i, j: (i, j)),
        out_shape=jax.ShapeDtypeStruct((m, n), BF16),
        scratch_shapes=[pltpu.VMEM((tm, d), BF16)],
        compiler_params=pltpu.CompilerParams(
            dimension_semantics=("arbitrary", "arbitrary"), vmem_limit_bytes=_vmem_limit(est)),
        name="in_proj",
    )(x2d, g.reshape(1, d), _col_tiles(w, tn))


def _rotate(x, cos, sin_lo, sin_hi):
    half = ROT_DIM // 2
    return (x * cos + pltpu.roll(x, HEAD_DIM - half, 1) * sin_lo + pltpu.roll(x, half, 1) * sin_hi)


def _attn_kernel(invf_ref, q_ref, k_ref, v_ref, g_ref, o_ref,
                 cos_ref, slo_ref, shi_ref, qb_ref, kb_ref, vt_ref, s_ref, p_ref, *, seq):
    nb = seq // MOBA_BLOCK
    half = ROT_DIM // 2
    blk = MOBA_BLOCK
    nt_dims = (((1,), (1,)), ((), ()))

    @pl.when((pl.program_id(0) == 0) & (pl.program_id(1) == 0))
    def _():
        pos = lax.broadcasted_iota(jnp.int32, (seq, HEAD_DIM), 0).astype(F32)
        lane = lax.broadcasted_iota(jnp.int32, (seq, HEAD_DIM), 1)
        ang = pos * invf_ref[...]
        sin = jnp.sin(ang)
        cos_ref[...] = jnp.cos(ang)
        slo_ref[...] = jnp.where(lane < half, -sin, 0.0)
        shi_ref[...] = jnp.where(lane >= half, sin, 0.0)

    cos, slo, shi = cos_ref[...], slo_ref[...], shi_ref[...]
    qr = _rotate(q_ref[...].astype(F32), cos, slo, shi)
    kr = _rotate(k_ref[...].astype(F32), cos, slo, shi)
    qb_ref[...] = qr.astype(BF16)
    kb_ref[...] = kr.astype(BF16)
    vt_ref[...] = v_ref[...].T

    kmean = jnp.mean(kr.reshape(nb, blk, HEAD_DIM), axis=1)
    gate = lax.dot_general(kmean, qr, nt_dims, precision=lax.Precision.HIGHEST,
                           preferred_element_type=F32)

    c = HEAD_DIM ** -0.5 * 1.4426950408889634
    key_i = lax.broadcasted_iota(jnp.int32, (blk, blk), 0)
    qry_i = lax.broadcasted_iota(jnp.int32, (blk, blk), 1)
    causal = key_i <= qry_i
    blk_i = lax.broadcasted_iota(jnp.int32, (nb, blk), 0)
    gain = g_ref[...]

    for i in range(nb):
        qcols = slice(i * blk, (i + 1) * blk)
        klen = (i + 1) * blk
        qi = qb_ref[qcols, :]

        drop = None
        if i > MOBA_TOPK:
            g = gate[:, qcols]
            rank = jnp.zeros((nb, blk), jnp.int32)
            for jj in range(i):
                gj = jnp.broadcast_to(g[jj:jj + 1, :], (nb, blk))
                beats = (gj > g) | ((gj == g) & (jj < blk_i))
                rank = rank + beats.astype(jnp.int32)
            drop = jnp.where((rank < MOBA_TOPK) & (blk_i < i), 0.0, -jnp.inf)

        m_run = None
        for j in range(i + 1):
            krows = slice(j * blk, (j + 1) * blk)
            s = lax.dot_general(kb_ref[krows, :], qi, nt_dims, preferred_element_type=F32) * c
            if j == i:
                s = jnp.where(causal, s, -jnp.inf)
            elif drop is not None:
                s = s + drop[j:j + 1, :]
            s_ref[krows, :] = s
            m_run = s if m_run is None else jnp.maximum(m_run, s)
        m = jnp.max(m_run, axis=0, keepdims=True)

        l_run = None
        for j in range(i + 1):
            krows = slice(j * blk, (j + 1) * blk)
            p = jnp.exp2(s_ref[krows, :] - m)
            p_ref[krows, :] = p.astype(BF16)
            l_run = p if l_run is None else l_run + p
        denom = jnp.sum(l_run, axis=0, keepdims=True)

        o = jnp.dot(vt_ref[:, 0:klen], p_ref[0:klen, :], preferred_element_type=F32) / denom
        o = o * lax.rsqrt(jnp.mean(o * o, axis=0, keepdims=True) + EPS) * gain
        o_ref[qcols, :] = o.T.astype(o_ref.dtype)


def _moba_attn(u3, attn_g, n_heads):
    b, s, _ = u3.shape
    half = ROT_DIM // 2
    inv_freq = ROPE_THETA ** (-jnp.arange(half, dtype=F32) / half)
    invf = jnp.concatenate([inv_freq, inv_freq, jnp.zeros((HEAD_DIM - ROT_DIM,), F32)]).reshape(1, HEAD_DIM)
    blk = lambda off: pl.BlockSpec((None, s, HEAD_DIM), lambda bi, hi: (bi, 0, off + hi))
    est = (3 * 2 * s * HEAD_DIM * 2 + 2 * s * HEAD_DIM * 2 + 3 * s * HEAD_DIM * 4 + 3 * s * HEAD_DIM * 2
           + s * MOBA_BLOCK * (4 + 2) + 4 * s * HEAD_DIM * 4)
    return pl.pallas_call(
        functools.partial(_attn_kernel, seq=s),
        grid=(b, n_heads),
        in_specs=[
            pl.BlockSpec((1, HEAD_DIM), lambda bi, hi: (0, 0)),
            blk(0), blk(n_heads), blk(2 * n_heads),
            pl.BlockSpec((HEAD_DIM, 1), lambda bi, hi: (hi, 0)),
        ],
        out_specs=pl.BlockSpec((None, s, HEAD_DIM), lambda bi, hi: (bi, 0, hi)),
        out_shape=jax.ShapeDtypeStruct((b, s, n_heads * HEAD_DIM), BF16),
        scratch_shapes=[
            pltpu.VMEM((s, HEAD_DIM), F32), pltpu.VMEM((s, HEAD_DIM), F32), pltpu.VMEM((s, HEAD_DIM), F32),
            pltpu.VMEM((s, HEAD_DIM), BF16), pltpu.VMEM((s, HEAD_DIM), BF16),
            pltpu.VMEM((HEAD_DIM, s), BF16),
            pltpu.VMEM((s, MOBA_BLOCK), F32), pltpu.VMEM((s, MOBA_BLOCK), BF16),
        ],
        compiler_params=pltpu.CompilerParams(
            dimension_semantics=("arbitrary", "arbitrary"), vmem_limit_bytes=_vmem_limit(est)),
        name="moba_attn",
    )(invf, u3, u3, u3, attn_g.reshape(n_heads * HEAD_DIM, 1))


CONV_PHASES = V7X_SUBLANES
CONV_BACK = (CONV_KERNEL - 1) // V7X_SUBLANES * V7X_SUBLANES
CONV_LANES = 128


def _conv_kernel(cv_ref, cg_ref, w_ref, b_ref, lg_ref, lb_ref, o_ref, gp_ref, gs_ref, wb_ref,
                 *, t_rows, chunk):
    width = gp_ref.shape[1]
    sub = V7X_SUBLANES

    @pl.when((pl.program_id(0) == 0) & (pl.program_id(1) == 0))
    def _():
        for k in range(CONV_KERNEL):
            wb_ref[k] = jnp.broadcast_to(w_ref[k:k + 1, :], (sub, width))
        wb_ref[CONV_KERNEL] = jnp.broadcast_to(b_ref[...], (sub, width))

    @pl.when(pl.program_id(1) == 0)
    def _():
        gp_ref[0:CONV_HALO, :] = jnp.zeros((CONV_HALO, width), F32)

    @pl.when(pl.program_id(1) > 0)
    def _():
        gp_ref[0:CONV_HALO, :] = gp_ref[t_rows:t_rows + CONV_HALO, :]

    gp_ref[CONV_HALO:, :] = cv_ref[...].astype(F32) * jax.nn.sigmoid(cg_ref[...].astype(F32))

    n_shift = t_rows + CONV_BACK
    for r in range(1, CONV_PHASES):
        gs_ref[r - 1] = gp_ref[sub - r:sub - r + n_shift, :]

    lg, lb = lg_ref[...], lb_ref[...]
    groups = chunk // sub

    def conv_chunk(c, carry):
        t0 = pl.multiple_of(c * chunk, chunk)
        tiles = []
        for l0 in range(0, width, CONV_LANES):
            cols = slice(l0, l0 + CONV_LANES)
            acc = jnp.broadcast_to(wb_ref[CONV_KERNEL, :, cols][None], (groups, sub, CONV_LANES))
            for k in range(CONV_KERNEL):
                back, r = divmod(CONV_KERNEL - 1 - k, sub)
                if r == 0:
                    src = gp_ref[pl.ds(CONV_HALO - sub * back + t0, chunk), cols]
                else:
                    src = gs_ref[r - 1, pl.ds(CONV_HALO - sub - sub * back + t0, chunk), cols]
                acc = acc + src.reshape(groups, sub, CONV_LANES) * wb_ref[k, :, cols][None]
            tiles.append(acc.reshape(chunk, CONV_LANES))
        acc = jnp.concatenate(tiles, axis=1)
        mu = jnp.mean(acc, axis=-1, keepdims=True)
        xc = acc - mu
        var = jnp.mean(xc * xc, axis=-1, keepdims=True)
        y = xc * lax.rsqrt(var + EPS) * lg + lb
        o_ref[pl.ds(t0, chunk), :] = (y * jax.nn.sigmoid(y)).astype(o_ref.dtype)
        return carry

    lax.fori_loop(0, t_rows // chunk, conv_chunk, 0)


def _conv_mod(u3, col_off, width, w, bias, ln_g, ln_b, t_rows=256, chunk=64):
    b, s, _ = u3.shape
    cblk = col_off // width
    assert CONV_HALO >= CONV_KERNEL - 1 and CONV_HALO - V7X_SUBLANES >= CONV_BACK
    n_shift = t_rows + CONV_BACK
    tile_f32 = t_rows * width * 4
    est = (2 * 2 * t_rows * width * 2 + 2 * t_rows * width * 2 + (t_rows + CONV_HALO) * width * 4
           + (CONV_PHASES - 1) * n_shift * width * 4 + (CONV_KERNEL + 1) * V7X_SUBLANES * width * 4
           + 6 * tile_f32)
    vec = lambda a: a.reshape(1, width)
    vspec = pl.BlockSpec((1, width), lambda bi, ti: (0, 0))
    return pl.pallas_call(
        functools.partial(_conv_kernel, t_rows=t_rows, chunk=chunk),
        grid=(b, s // t_rows),
        in_specs=[
            pl.BlockSpec((None, t_rows, width), lambda bi, ti: (bi, ti, cblk)),
            pl.BlockSpec((None, t_rows, width), lambda bi, ti: (bi, ti, cblk + 1)),
            pl.BlockSpec((CONV_KERNEL, width), lambda bi, ti: (0, 0)),
            vspec, vspec, vspec,
        ],
        out_specs=pl.BlockSpec((None, t_rows, width), lambda bi, ti: (bi, ti, 0)),
        out_shape=jax.ShapeDtypeStruct((b, s, width), BF16),
        scratch_shapes=[
            pltpu.VMEM((t_rows + CONV_HALO, width), F32),
            pltpu.VMEM((CONV_PHASES - 1, n_shift, width), F32),
            pltpu.VMEM((CONV_KERNEL + 1, V7X_SUBLANES, width), F32),
        ],
        compiler_params=pltpu.CompilerParams(
            dimension_semantics=("arbitrary", "arbitrary"), vmem_limit_bytes=_vmem_limit(est)),
        name="conv_mod",
    )(u3, u3, w, vec(bias), vec(ln_g), vec(ln_b))


def _out_proj_kernel(x_ref, a_ref, c_ref, wa_ref, wc_ref, g_ref, x1_ref, hf_ref):
    y = (jnp.dot(a_ref[...], wa_ref[...], preferred_element_type=F32)
         + jnp.dot(c_ref[...], wc_ref[...], preferred_element_type=F32))
    x1 = x_ref[...] + y
    x1_ref[...] = x1
    hf_ref[...] = _rms(x1, g_ref[...]).astype(hf_ref.dtype)


def _out_proj(x2d, a2d, c2d, w_out, g, tm=256):
    m, d = x2d.shape
    ka, kc = a2d.shape[1], c2d.shape[1]
    assert ka == kc and ka + kc == w_out.shape[0]
    est = 2 * tm * d * 4 * 2 + 2 * tm * d * 2 + 2 * 2 * tm * ka * 2 + 2 * 2 * ka * d * 2
    return pl.pallas_call(
        _out_proj_kernel,
        grid=(m // tm,),
        in_specs=[
            pl.BlockSpec((tm, d), lambda i: (i, 0)),
            pl.BlockSpec((tm, ka), lambda i: (i, 0)),
            pl.BlockSpec((tm, kc), lambda i: (i, 0)),
            pl.BlockSpec((ka, d), lambda i: (0, 0)),
            pl.BlockSpec((kc, d), lambda i: (1, 0)),
            pl.BlockSpec((1, d), lambda i: (0, 0)),
        ],
        out_specs=[pl.BlockSpec((tm, d), lambda i: (i, 0)), pl.BlockSpec((tm, d), lambda i: (i, 0))],
        out_shape=[jax.ShapeDtypeStruct((m, d), F32), jax.ShapeDtypeStruct((m, d), BF16)],
        compiler_params=pltpu.CompilerParams(
            dimension_semantics=("arbitrary",), vmem_limit_bytes=_vmem_limit(est)),
        name="out_proj",
    )(x2d, a2d, c2d, w_out, w_out, g.reshape(1, d))


FFN_PIPE_DEPTH = 1
FFN_ROW_CHUNK = 64
FFN_LANE_CHUNK = 128
FFN_COL_CHUNK = 512


def _ffn_kernel(hf_ref, x1_ref, wua_ref, wub_ref, cwa_ref, cwb_ref, cba_ref, cbb_ref, wd_ref, gf_ref,
                o_ref, acc_ref, sa0_ref, sb0_ref, sa1_ref, sb1_ref, act_ref,
                carry_a_ref, carry_b_ref, *, tm, nf, n_items, tiles_per_seq, final_norm):
    t = pl.program_id(0)
    ta = jnp.minimum(t, n_items - 1)
    fa = ta % nf
    seq_start = ((ta // nf) % tiles_per_seq) == 0
    d = acc_ref.shape[1]

    @pl.when(t == 0)
    def _():
        for ref in (acc_ref, sa1_ref, sb1_ref, carry_a_ref, carry_b_ref):
            ref[...] = jnp.zeros(ref.shape, ref.dtype)

    def body(s_w, s_r):
        hf = hf_ref[...]

        def up_half(wu_ref, sw_ref, carry_ref):
            sw_ref[0:FFN_HALO, :] = jnp.where(seq_start, 0.0, carry_ref[fa])
            sw_ref[FFN_HALO:, :] = jnp.dot(hf, wu_ref[...], preferred_element_type=F32)
            carry_ref[fa] = sw_ref[tm:tm + FFN_HALO, :]

        up_half(wua_ref, s_w[0], carry_a_ref)

        def taps(cw_ref, cb_ref, cols):
            bc = lambda v: jnp.broadcast_to(v, (FFN_ROW_CHUNK, FFN_LANE_CHUNK))
            return [bc(cw_ref[k:k + 1, cols]) for k in range(FFN_CONV_KERNEL)], bc(cb_ref[:, cols])

        def conv(sr_ref, w, bias, r0, cols):
            y = bias
            for k in range(FFN_CONV_KERNEL):
                off = r0 + FFN_HALO - (FFN_CONV_KERNEL - 1) + k
                y = y + sr_ref[off:off + FFN_ROW_CHUNK, cols] * w[k]
            return y

        for c0 in range(0, act_ref.shape[1], FFN_LANE_CHUNK):
            cols = slice(c0, c0 + FFN_LANE_CHUNK)
            wa, ba = taps(cwa_ref, cba_ref, cols)
            wb, bb = taps(cwb_ref, cbb_ref, cols)
            for r0 in range(0, tm, FFN_ROW_CHUNK):
                ya = conv(s_r[0], wa, ba, r0, cols)
                yb = conv(s_r[1], wb, bb, r0, cols)
                act_ref[r0:r0 + FFN_ROW_CHUNK, cols] = (ya * jax.nn.sigmoid(ya) * yb).astype(BF16)

        act = act_ref[...]
        for c0 in range(0, d, FFN_COL_CHUNK):
            acc_ref[:, c0:c0 + FFN_COL_CHUNK] += jnp.dot(
                act, wd_ref[:, c0:c0 + FFN_COL_CHUNK], preferred_element_type=F32)

        up_half(wub_ref, s_w[1], carry_b_ref)

    @pl.when(t % 2 == 0)
    def _():
        body((sa0_ref, sb0_ref), (sa1_ref, sb1_ref))

    @pl.when(t % 2 == 1)
    def _():
        body((sa1_ref, sb1_ref), (sa0_ref, sb0_ref))

    @pl.when(t == 0)
    def _():
        acc_ref[...] = jnp.zeros(acc_ref.shape, F32)

    @pl.when((t >= FFN_PIPE_DEPTH) & ((t - FFN_PIPE_DEPTH) % nf == nf - 1))
    def _():
        x2 = x1_ref[...] + acc_ref[...]
        o_ref[...] = _rms(x2, gf_ref[...]) if final_norm else x2
        acc_ref[...] = jnp.zeros(acc_ref.shape, F32)


def _conv_ffn(hf, x1, w_up, cw, cb, w_down, g_final, seq, final_norm, tm=512, tf=512):
    m, d = hf.shape
    dff = w_down.shape[0]
    nf = dff // tf
    n_items = (m // tm) * nf
    s_bytes = (tm + FFN_HALO) * tf * 4
    est = (2 * tm * d * 2 + 2 * tm * d * 4 * 2 + tm * d * 4 + 2 * 2 * d * tf * 2 + 2 * tf * d * 2
           + 4 * s_bytes + tm * tf * 2 + 2 * nf * FFN_HALO * tf * 4 + 4 * tm * tf * 4)

    item = lambda t, lag: jnp.clip(t - lag, 0, n_items - 1)
    row_a = lambda t: (item(t, 0) // nf, 0)
    row_c = lambda t: (item(t, FFN_PIPE_DEPTH) // nf, 0)
    vspec = lambda r, half: pl.BlockSpec((r, tf), lambda t: (0, half * nf + item(t, FFN_PIPE_DEPTH) % nf))
    cb2 = cb.reshape(1, 2 * dff)
    w_up_tiles = _col_tiles(w_up, tf)
    s_scratch = pltpu.VMEM((tm + FFN_HALO, tf), F32)
    return pl.pallas_call(
        functools.partial(_ffn_kernel, tm=tm, nf=nf, n_items=n_items, tiles_per_seq=seq // tm,
                          final_norm=final_norm),
        grid=(n_items + FFN_PIPE_DEPTH,),
        in_specs=[
            pl.BlockSpec((tm, d), row_a),
            pl.BlockSpec((tm, d), row_c),
            pl.BlockSpec((None, d, tf), lambda t: (item(t, 0) % nf, 0, 0)),
            pl.BlockSpec((None, d, tf), lambda t: (nf + item(t, 0) % nf, 0, 0)),
            vspec(FFN_CONV_KERNEL, 0), vspec(FFN_CONV_KERNEL, 1), vspec(1, 0), vspec(1, 1),
            pl.BlockSpec((tf, d), lambda t: (item(t, FFN_PIPE_DEPTH) % nf, 0)),
            pl.BlockSpec((1, d), lambda t: (0, 0)),
        ],
        out_specs=pl.BlockSpec((tm, d), row_c),
        out_shape=jax.ShapeDtypeStruct((m, d), F32),
        scratch_shapes=[
            pltpu.VMEM((tm, d), F32),
            s_scratch, s_scratch, s_scratch, s_scratch,
            pltpu.VMEM((tm, tf), BF16),
            pltpu.VMEM((nf, FFN_HALO, tf), F32),
            pltpu.VMEM((nf, FFN_HALO, tf), F32),
        ],
        compiler_params=pltpu.CompilerParams(
            dimension_semantics=("arbitrary",), vmem_limit_bytes=_vmem_limit(est)),
        name="conv_ffn",
    )(hf, x1, w_up_tiles, w_up_tiles, cw, cw, cb2, cb2, w_down.astype(BF16), g_final.reshape(1, d))


def kernel(x, norm_mix_g, w_in, attn_out_g, conv_dw_w, conv_dw_b, conv_ln_g, conv_ln_b,
           w_out, norm_ffn_g, w_up, ffn_dw_w, ffn_dw_b, w_down, norm_final_g):
    b, s, d = x.shape
    depth = w_in.shape[0]
    attn_width = attn_out_g.shape[1]
    conv_width = conv_dw_b.shape[1]
    n_heads = attn_width // HEAD_DIM
    assert s % MOBA_BLOCK == 0 and w_in.shape[2] == 3 * attn_width + 2 * conv_width

    xf = x.reshape(b * s, d)
    for l in range(depth):
        u = _in_proj(xf, norm_mix_g[l], w_in[l])
        u3 = u.reshape(b, s, -1)
        a = _moba_attn(u3, attn_out_g[l], n_heads)
        c = _conv_mod(u3, 3 * attn_width, conv_width, conv_dw_w[l], conv_dw_b[l], conv_ln_g[l], conv_ln_b[l])
        x1, hf = _out_proj(xf, a.reshape(b * s, attn_width), c.reshape(b * s, conv_width),
                           w_out[l].astype(BF16), norm_ffn_g[l])
        xf = _conv_ffn(hf, x1, w_up[l], ffn_dw_w[l], ffn_dw_b[l], w_down[l],
                       norm_final_g, s, final_norm=(l == depth - 1))
    return xf.reshape(b, s, d)
```

```python
import functools

import jax
import jax.numpy as jnp
from jax import lax
from jax.experimental import pallas as pl
from jax.experimental.pallas import tpu as pltpu

F32 = jnp.float32
BF16 = jnp.bfloat16

HEAD_DIM = 128
ROT_DIM = HEAD_DIM // 4
ROPE_THETA = 500000.0
MOBA_BLOCK = 256
MOBA_TOPK = 3
CONV_KERNEL = 31
FFN_CONV_KERNEL = 3
EPS = 1e-5

V7X_SUBLANES = 8
V7X_VMEM_BYTES = 64 * 1024 * 1024

CONV_HALO = 32
FFN_HALO = V7X_SUBLANES


def _vmem_limit(nbytes):
    return int(min(nbytes * 1.5 + (8 << 20), V7X_VMEM_BYTES - (4 << 20)))


def _rms(x, g):
    return x * lax.rsqrt(jnp.mean(x * x, axis=-1, keepdims=True) + EPS) * g


def _in_proj_kernel(x_ref, g_ref, w_ref, o_ref, hn_ref):
    @pl.when(pl.program_id(1) == 0)
    def _():
        hn_ref[...] = _rms(x_ref[...], g_ref[...]).astype(BF16)

    o_ref[...] = jnp.dot(hn_ref[...], w_ref[...], preferred_element_type=F32).astype(o_ref.dtype)


def _in_proj(x2d, g, w, tm=512, tn=1024):
    m, d = x2d.shape
    n = w.shape[1]
    est = 2 * tm * d * 4 + tm * d * 2 + 2 * d * tn * 2 + 2 * tm * tn * 2
    return pl.pallas_call(
        _in_proj_kernel,
        grid=(m // tm, n // tn),
        in_specs=[
            pl.BlockSpec((tm, d), lambda i, j: (i, 0)),
            pl.BlockSpec((1, d), lambda i, j: (0, 0)),
            pl.BlockSpec((d, tn), lambda i, j: (0, j)),
        ],
        out_specs=pl.BlockSpec((tm, tn), lambda i, j: (i, j)),
        out_shape=jax.ShapeDtypeStruct((m, n), BF16),
        scratch_shapes=[pltpu.VMEM((tm, d), BF16)],
        compiler_params=pltpu.CompilerParams(
            dimension_semantics=("arbitrary", "arbitrary"), vmem_limit_bytes=_vmem_limit(est)),
        name="in_proj",
    )(x2d, g.reshape(1, d), w.astype(BF16))


def _rotate(x, cos, sin_lo, sin_hi):
    half = ROT_DIM // 2
    return (x * cos + pltpu.roll(x, HEAD_DIM - half, 1) * sin_lo + pltpu.roll(x, half, 1) * sin_hi)


def _attn_kernel(invf_ref, q_ref, k_ref, v_ref, g_ref, o_ref,
                 cos_ref, slo_ref, shi_ref, qb_ref, kb_ref, vt_ref, s_ref, p_ref, *, seq):
    nb = seq // MOBA_BLOCK
    half = ROT_DIM // 2
    blk = MOBA_BLOCK
    nt_dims = (((1,), (1,)), ((), ()))

    @pl.when((pl.program_id(0) == 0) & (pl.program_id(1) == 0))
    def _():
        pos = lax.broadcasted_iota(jnp.int32, (seq, HEAD_DIM), 0).astype(F32)
        lane = lax.broadcasted_iota(jnp.int32, (seq, HEAD_DIM), 1)
        ang = pos * invf_ref[...]
        sin = jnp.sin(ang)
        cos_ref[...] = jnp.cos(ang)
        slo_ref[...] = jnp.where(lane < half, -sin, 0.0)
        shi_ref[...] = jnp.where(lane >= half, sin, 0.0)

    cos, slo, shi = cos_ref[...], slo_ref[...], shi_ref[...]
    qr = _rotate(q_ref[...].astype(F32), cos, slo, shi)
    kr = _rotate(k_ref[...].astype(F32), cos, slo, shi)
    qb_ref[...] = qr.astype(BF16)
    kb_ref[...] = kr.astype(BF16)
    vt_ref[...] = v_ref[...].T

    kmean = jnp.mean(kr.reshape(nb, blk, HEAD_DIM), axis=1)
    gate = lax.dot_general(kmean, qr, nt_dims, precision=lax.Precision.HIGHEST,
                           preferred_element_type=F32)

    c = HEAD_DIM ** -0.5 * 1.4426950408889634
    key_i = lax.broadcasted_iota(jnp.int32, (blk, blk), 0)
    qry_i = lax.broadcasted_iota(jnp.int32, (blk, blk), 1)
    causal = key_i <= qry_i
    blk_i = lax.broadcasted_iota(jnp.int32, (nb, blk), 0)
    gain = g_ref[...]

    for i in range(nb):
        qcols = slice(i * blk, (i + 1) * blk)
        klen = (i + 1) * blk
        qi = qb_ref[qcols, :]

        drop = None
        if i > MOBA_TOPK:
            g = gate[:, qcols]
            rank = jnp.zeros((nb, blk), jnp.int32)
            for jj in range(i):
                gj = jnp.broadcast_to(g[jj:jj + 1, :], (nb, blk))
                beats = (gj > g) | ((gj == g) & (jj < blk_i))
                rank = rank + beats.astype(jnp.int32)
            drop = jnp.where((rank < MOBA_TOPK) & (blk_i < i), 0.0, -jnp.inf)

        m_run = None
        for j in range(i + 1):
            krows = slice(j * blk, (j + 1) * blk)
            s = lax.dot_general(kb_ref[krows, :], qi, nt_dims, preferred_element_type=F32) * c
            if j == i:
                s = jnp.where(causal, s, -jnp.inf)
            elif drop is not None:
                s = s + drop[j:j + 1, :]
            s_ref[krows, :] = s
            m_run = s if m_run is None else jnp.maximum(m_run, s)
        m = jnp.max(m_run, axis=0, keepdims=True)

        l_run = None
        for j in range(i + 1):
            krows = slice(j * blk, (j + 1) * blk)
            p = jnp.exp2(s_ref[krows, :] - m)
            p_ref[krows, :] = p.astype(BF16)
            l_run = p if l_run is None else l_run + p
        denom = jnp.sum(l_run, axis=0, keepdims=True)

        o = jnp.dot(vt_ref[:, 0:klen], p_ref[0:klen, :], preferred_element_type=F32) / denom
        o = o * lax.rsqrt(jnp.mean(o * o, axis=0, keepdims=True) + EPS) * gain
        o_ref[qcols, :] = o.T.astype(o_ref.dtype)


def _moba_attn(u3, attn_g, n_heads):
    b, s, _ = u3.shape
    half = ROT_DIM // 2
    inv_freq = ROPE_THETA ** (-jnp.arange(half, dtype=F32) / half)
    invf = jnp.concatenate([inv_freq, inv_freq, jnp.zeros((HEAD_DIM - ROT_DIM,), F32)]).reshape(1, HEAD_DIM)
    blk = lambda off: pl.BlockSpec((None, s, HEAD_DIM), lambda bi, hi: (bi, 0, off + hi))
    est = (3 * 2 * s * HEAD_DIM * 2 + 2 * s * HEAD_DIM * 2 + 3 * s * HEAD_DIM * 4 + 3 * s * HEAD_DIM * 2
           + s * MOBA_BLOCK * (4 + 2) + 4 * s * HEAD_DIM * 4)
    return pl.pallas_call(
        functools.partial(_attn_kernel, seq=s),
        grid=(b, n_heads),
        in_specs=[
            pl.BlockSpec((1, HEAD_DIM), lambda bi, hi: (0, 0)),
            blk(0), blk(n_heads), blk(2 * n_heads),
            pl.BlockSpec((HEAD_DIM, 1), lambda bi, hi: (hi, 0)),
        ],
        out_specs=pl.BlockSpec((None, s, HEAD_DIM), lambda bi, hi: (bi, 0, hi)),
        out_shape=jax.ShapeDtypeStruct((b, s, n_heads * HEAD_DIM), BF16),
        scratch_shapes=[
            pltpu.VMEM((s, HEAD_DIM), F32), pltpu.VMEM((s, HEAD_DIM), F32), pltpu.VMEM((s, HEAD_DIM), F32),
            pltpu.VMEM((s, HEAD_DIM), BF16), pltpu.VMEM((s, HEAD_DIM), BF16),
            pltpu.VMEM((HEAD_DIM, s), BF16),
            pltpu.VMEM((s, MOBA_BLOCK), F32), pltpu.VMEM((s, MOBA_BLOCK), BF16),
        ],
        compiler_params=pltpu.CompilerParams(
            dimension_semantics=("arbitrary", "arbitrary"), vmem_limit_bytes=_vmem_limit(est)),
        name="moba_attn",
    )(invf, u3, u3, u3, attn_g.reshape(n_heads * HEAD_DIM, 1))


CONV_PHASES = V7X_SUBLANES
CONV_BACK = (CONV_KERNEL - 1) // V7X_SUBLANES * V7X_SUBLANES
CONV_LANES = 128


def _conv_kernel(cv_ref, cg_ref, w_ref, b_ref, lg_ref, lb_ref, o_ref, gp_ref, gs_ref, wb_ref,
                 *, t_rows, chunk):
    width = gp_ref.shape[1]
    sub = V7X_SUBLANES

    @pl.when((pl.program_id(0) == 0) & (pl.program_id(1) == 0))
    def _():
        for k in range(CONV_KERNEL):
            wb_ref[k] = jnp.broadcast_to(w_ref[k:k + 1, :], (sub, width))
        wb_ref[CONV_KERNEL] = jnp.broadcast_to(b_ref[...], (sub, width))

    @pl.when(pl.program_id(1) == 0)
    def _():
        gp_ref[0:CONV_HALO, :] = jnp.zeros((CONV_HALO, width), F32)

    @pl.when(pl.program_id(1) > 0)
    def _():
        gp_ref[0:CONV_HALO, :] = gp_ref[t_rows:t_rows + CONV_HALO, :]

    gp_ref[CONV_HALO:, :] = cv_ref[...].astype(F32) * jax.nn.sigmoid(cg_ref[...].astype(F32))

    n_shift = t_rows + CONV_BACK
    for r in range(1, CONV_PHASES):
        gs_ref[r - 1] = gp_ref[sub - r:sub - r + n_shift, :]

    lg, lb = lg_ref[...], lb_ref[...]
    groups = chunk // sub

    def conv_chunk(c, carry):
        t0 = pl.multiple_of(c * chunk, chunk)
        tiles = []
        for l0 in range(0, width, CONV_LANES):
            cols = slice(l0, l0 + CONV_LANES)
            acc = jnp.broadcast_to(wb_ref[CONV_KERNEL, :, cols][None], (groups, sub, CONV_LANES))
            for k in range(CONV_KERNEL):
                back, r = divmod(CONV_KERNEL - 1 - k, sub)
                if r == 0:
                    src = gp_ref[pl.ds(CONV_HALO - sub * back + t0, chunk), cols]
                else:
                    src = gs_ref[r - 1, pl.ds(CONV_HALO - sub - sub * back + t0, chunk), cols]
                acc = acc + src.reshape(groups, sub, CONV_LANES) * wb_ref[k, :, cols][None]
            tiles.append(acc.reshape(chunk, CONV_LANES))
        acc = jnp.concatenate(tiles, axis=1)
        mu = jnp.mean(acc, axis=-1, keepdims=True)
        xc = acc - mu
        var = jnp.mean(xc * xc, axis=-1, keepdims=True)
        y = xc * lax.rsqrt(var + EPS) * lg + lb
        o_ref[pl.ds(t0, chunk), :] = (y * jax.nn.sigmoid(y)).astype(o_ref.dtype)
        return carry

    lax.fori_loop(0, t_rows // chunk, conv_chunk, 0)


def _conv_mod(u3, col_off, width, w, bias, ln_g, ln_b, t_rows=256, chunk=64):
    b, s, _ = u3.shape
    cblk = col_off // width
    assert CONV_HALO >= CONV_KERNEL - 1 and CONV_HALO - V7X_SUBLANES >= CONV_BACK
    n_shift = t_rows + CONV_BACK
    tile_f32 = t_rows * width * 4
    est = (2 * 2 * t_rows * width * 2 + 2 * t_rows * width * 2 + (t_rows + CONV_HALO) * width * 4
           + (CONV_PHASES - 1) * n_shift * width * 4 + (CONV_KERNEL + 1) * V7X_SUBLANES * width * 4
           + 6 * tile_f32)
    vec = lambda a: a.reshape(1, width)
    vspec = pl.BlockSpec((1, width), lambda bi, ti: (0, 0))
    return pl.pallas_call(
        functools.partial(_conv_kernel, t_rows=t_rows, chunk=chunk),
        grid=(b, s // t_rows),
        in_specs=[
            pl.BlockSpec((None, t_rows, width), lambda bi, ti: (bi, ti, cblk)),
            pl.BlockSpec((None, t_rows, width), lambda bi, ti: (bi, ti, cblk + 1)),
            pl.BlockSpec((CONV_KERNEL, width), lambda bi, ti: (0, 0)),
            vspec, vspec, vspec,
        ],
        out_specs=pl.BlockSpec((None, t_rows, width), lambda bi, ti: (bi, ti, 0)),
        out_shape=jax.ShapeDtypeStruct((b, s, width), BF16),
        scratch_shapes=[
            pltpu.VMEM((t_rows + CONV_HALO, width), F32),
            pltpu.VMEM((CONV_PHASES - 1, n_shift, width), F32),
            pltpu.VMEM((CONV_KERNEL + 1, V7X_SUBLANES, width), F32),
        ],
        compiler_params=pltpu.CompilerParams(
            dimension_semantics=("arbitrary", "arbitrary"), vmem_limit_bytes=_vmem_limit(est)),
        name="conv_mod",
    )(u3, u3, w, vec(bias), vec(ln_g), vec(ln_b))


def _out_proj_kernel(x_ref, a_ref, c_ref, wa_ref, wc_ref, g_ref, x1_ref, hf_ref):
    y = (jnp.dot(a_ref[...], wa_ref[...], preferred_element_type=F32)
         + jnp.dot(c_ref[...], wc_ref[...], preferred_element_type=F32))
    x1 = x_ref[...] + y
    x1_ref[...] = x1
    hf_ref[...] = _rms(x1, g_ref[...]).astype(hf_ref.dtype)


def _out_proj(x2d, a2d, c2d, w_out, g, tm=256):
    m, d = x2d.shape
    ka, kc = a2d.shape[1], c2d.shape[1]
    assert ka == kc and ka + kc == w_out.shape[0]
    est = 2 * tm * d * 4 * 2 + 2 * tm * d * 2 + 2 * 2 * tm * ka * 2 + 2 * 2 * ka * d * 2
    return pl.pallas_call(
        _out_proj_kernel,
        grid=(m // tm,),
        in_specs=[
            pl.BlockSpec((tm, d), lambda i: (i, 0)),
            pl.BlockSpec((tm, ka), lambda i: (i, 0)),
            pl.BlockSpec((tm, kc), lambda i: (i, 0)),
            pl.BlockSpec((ka, d), lambda i: (0, 0)),
            pl.BlockSpec((kc, d), lambda i: (1, 0)),
            pl.BlockSpec((1, d), lambda i: (0, 0)),
        ],
        out_specs=[pl.BlockSpec((tm, d), lambda i: (i, 0)), pl.BlockSpec((tm, d), lambda i: (i, 0))],
        out_shape=[jax.ShapeDtypeStruct((m, d), F32), jax.ShapeDtypeStruct((m, d), BF16)],
        compiler_params=pltpu.CompilerParams(
            dimension_semantics=("arbitrary",), vmem_limit_bytes=_vmem_limit(est)),
        name="out_proj",
    )(x2d, a2d, c2d, w_out, w_out, g.reshape(1, d))


FFN_PIPE_DEPTH = 1
FFN_ROW_CHUNK = 64
FFN_LANE_CHUNK = 128
FFN_COL_CHUNK = 512


def _ffn_kernel(hf_ref, x1_ref, wua_ref, wub_ref, cwa_ref, cwb_ref, cba_ref, cbb_ref, wd_ref, gf_ref,
                o_ref, acc_ref, sa0_ref, sb0_ref, sa1_ref, sb1_ref, act_ref,
                carry_a_ref, carry_b_ref, *, tm, nf, n_items, tiles_per_seq, final_norm):
    t = pl.program_id(0)
    ta = jnp.minimum(t, n_items - 1)
    fa = ta % nf
    seq_start = ((ta // nf) % tiles_per_seq) == 0
    d = acc_ref.shape[1]

    @pl.when(t == 0)
    def _():
        for ref in (acc_ref, sa1_ref, sb1_ref, carry_a_ref, carry_b_ref):
            ref[...] = jnp.zeros(ref.shape, ref.dtype)

    def body(s_w, s_r):
        hf = hf_ref[...]

        def up_half(wu_ref, sw_ref, carry_ref):
            sw_ref[0:FFN_HALO, :] = jnp.where(seq_start, 0.0, carry_ref[fa])
            sw_ref[FFN_HALO:, :] = jnp.dot(hf, wu_ref[...], preferred_element_type=F32)
            carry_ref[fa] = sw_ref[tm:tm + FFN_HALO, :]

        up_half(wua_ref, s_w[0], carry_a_ref)

        def taps(cw_ref, cb_ref, cols):
            bc = lambda v: jnp.broadcast_to(v, (FFN_ROW_CHUNK, FFN_LANE_CHUNK))
            return [bc(cw_ref[k:k + 1, cols]) for k in range(FFN_CONV_KERNEL)], bc(cb_ref[:, cols])

        def conv(sr_ref, w, bias, r0, cols):
            y = bias
            for k in range(FFN_CONV_KERNEL):
                off = r0 + FFN_HALO - (FFN_CONV_KERNEL - 1) + k
                y = y + sr_ref[off:off + FFN_ROW_CHUNK, cols] * w[k]
            return y

        for c0 in range(0, act_ref.shape[1], FFN_LANE_CHUNK):
            cols = slice(c0, c0 + FFN_LANE_CHUNK)
            wa, ba = taps(cwa_ref, cba_ref, cols)
            wb, bb = taps(cwb_ref, cbb_ref, cols)
            for r0 in range(0, tm, FFN_ROW_CHUNK):
                ya = conv(s_r[0], wa, ba, r0, cols)
                yb = conv(s_r[1], wb, bb, r0, cols)
                act_ref[r0:r0 + FFN_ROW_CHUNK, cols] = (ya * jax.nn.sigmoid(ya) * yb).astype(BF16)

        act = act_ref[...]
        for c0 in range(0, d, FFN_COL_CHUNK):
            acc_ref[:, c0:c0 + FFN_COL_CHUNK] += jnp.dot(
                act, wd_ref[:, c0:c0 + FFN_COL_CHUNK], preferred_element_type=F32)

        up_half(wub_ref, s_w[1], carry_b_ref)

    @pl.when(t % 2 == 0)
    def _():
        body((sa0_ref, sb0_ref), (sa1_ref, sb1_ref))

    @pl.when(t % 2 == 1)
    def _():
        body((sa1_ref, sb1_ref), (sa0_ref, sb0_ref))

    @pl.when(t == 0)
    def _():
        acc_ref[...] = jnp.zeros(acc_ref.shape, F32)

    @pl.when((t >= FFN_PIPE_DEPTH) & ((t - FFN_PIPE_DEPTH) % nf == nf - 1))
    def _():
        x2 = x1_ref[...] + acc_ref[...]
        o_ref[...] = _rms(x2, gf_ref[...]) if final_norm else x2
        acc_ref[...] = jnp.zeros(acc_ref.shape, F32)


def _conv_ffn(hf, x1, w_up, cw, cb, w_down, g_final, seq, final_norm, tm=512, tf=512):
    m, d = hf.shape
    dff = w_down.shape[0]
    nf = dff // tf
    n_items = (m // tm) * nf
    s_bytes = (tm + FFN_HALO) * tf * 4
    est = (2 * tm * d * 2 + 2 * tm * d * 4 * 2 + tm * d * 4 + 2 * 2 * d * tf * 2 + 2 * tf * d * 2
           + 4 * s_bytes + tm * tf * 2 + 2 * nf * FFN_HALO * tf * 4 + 4 * tm * tf * 4)

    item = lambda t, lag: jnp.clip(t - lag, 0, n_items - 1)
    row_a = lambda t: (item(t, 0) // nf, 0)
    row_c = lambda t: (item(t, FFN_PIPE_DEPTH) // nf, 0)
    vspec = lambda r, half: pl.BlockSpec((r, tf), lambda t: (0, half * nf + item(t, FFN_PIPE_DEPTH) % nf))
    cb2 = cb.reshape(1, 2 * dff)
    w_up_bf = w_up.astype(BF16)
    s_scratch = pltpu.VMEM((tm + FFN_HALO, tf), F32)
    return pl.pallas_call(
        functools.partial(_ffn_kernel, tm=tm, nf=nf, n_items=n_items, tiles_per_seq=seq // tm,
                          final_norm=final_norm),
        grid=(n_items + FFN_PIPE_DEPTH,),
        in_specs=[
            pl.BlockSpec((tm, d), row_a),
            pl.BlockSpec((tm, d), row_c),
            pl.BlockSpec((d, tf), lambda t: (0, item(t, 0) % nf)),
            pl.BlockSpec((d, tf), lambda t: (0, nf + item(t, 0) % nf)),
            vspec(FFN_CONV_KERNEL, 0), vspec(FFN_CONV_KERNEL, 1), vspec(1, 0), vspec(1, 1),
            pl.BlockSpec((tf, d), lambda t: (item(t, FFN_PIPE_DEPTH) % nf, 0)),
            pl.BlockSpec((1, d), lambda t: (0, 0)),
        ],
        out_specs=pl.BlockSpec((tm, d), row_c),
        out_shape=jax.ShapeDtypeStruct((m, d), F32),
        scratch_shapes=[
            pltpu.VMEM((tm, d), F32),
            s_scratch, s_scratch, s_scratch, s_scratch,
            pltpu.VMEM((tm, tf), BF16),
            pltpu.VMEM((nf, FFN_HALO, tf), F32),
            pltpu.VMEM((nf, FFN_HALO, tf), F32),
        ],
        compiler_params=pltpu.CompilerParams(
            dimension_semantics=("arbitrary",), vmem_limit_bytes=_vmem_limit(est)),
        name="conv_ffn",
    )(hf, x1, w_up_bf, w_up_bf, cw, cw, cb2, cb2, w_down.astype(BF16), g_final.reshape(1, d))


def kernel(x, norm_mix_g, w_in, attn_out_g, conv_dw_w, conv_dw_b, conv_ln_g, conv_ln_b,
           w_out, norm_ffn_g, w_up, ffn_dw_w, ffn_dw_b, w_down, norm_final_g):
    b, s, d = x.shape
    depth = w_in.shape[0]
    attn_width = attn_out_g.shape[1]
    conv_width = conv_dw_b.shape[1]
    n_heads = attn_width // HEAD_DIM
    assert s % MOBA_BLOCK == 0 and w_in.shape[2] == 3 * attn_width + 2 * conv_width

    xf = x.reshape(b * s, d)
    for l in range(depth):
        u = _in_proj(xf, norm_mix_g[l], w_in[l])
        u3 = u.reshape(b, s, -1)
        a = _moba_attn(u3, attn_out_g[l], n_heads)
        c = _conv_mod(u3, 3 * attn_width, conv_width, conv_dw_w[l], conv_dw_b[l], conv_ln_g[l], conv_ln_b[l])
        x1, hf = _out_proj(xf, a.reshape(b * s, attn_width), c.reshape(b * s, conv_width),
                           w_out[l].astype(BF16), norm_ffn_g[l])
        xf = _conv_ffn(hf, x1, w_up[l], ffn_dw_w[l], ffn_dw_b[l], w_down[l],
                       norm_final_g, s, final_norm=(l == depth - 1))
    return xf.reshape(b, s, d)
```

```python
import functools

import jax
import jax.numpy as jnp
from jax import lax
from jax.experimental import pallas as pl
from jax.experimental.pallas import tpu as pltpu

F32 = jnp.float32
BF16 = jnp.bfloat16

HEAD_DIM = 128
ROT_DIM = HEAD_DIM // 4
ROPE_THETA = 500000.0
MOBA_BLOCK = 256
MOBA_TOPK = 3
CONV_KERNEL = 31
FFN_CONV_KERNEL = 3
EPS = 1e-5

V7X_SUBLANES = 8
V7X_VMEM_BYTES = 64 * 1024 * 1024

CONV_HALO = 32
FFN_HALO = V7X_SUBLANES


def _vmem_limit(nbytes):
    return int(min(nbytes * 1.5 + (8 << 20), V7X_VMEM_BYTES - (4 << 20)))


def _rms(x, g):
    return x * lax.rsqrt(jnp.mean(x * x, axis=-1, keepdims=True) + EPS) * g


def _in_proj_kernel(x_ref, g_ref, w_ref, o_ref, hn_ref):
    @pl.when(pl.program_id(1) == 0)
    def _():
        hn_ref[...] = _rms(x_ref[...], g_ref[...]).astype(BF16)

    o_ref[...] = jnp.dot(hn_ref[...], w_ref[...], preferred_element_type=F32).astype(o_ref.dtype)


def _in_proj(x2d, g, w, tm=1024, tn=1024):
    m, d = x2d.shape
    n = w.shape[1]
    est = 2 * tm * d * 4 + tm * d * 2 + 2 * d * tn * 2 + 2 * tm * tn * 2
    return pl.pallas_call(
        _in_proj_kernel,
        grid=(m // tm, n // tn),
        in_specs=[
            pl.BlockSpec((tm, d), lambda i, j: (i, 0)),
            pl.BlockSpec((1, d), lambda i, j: (0, 0)),
            pl.BlockSpec((d, tn), lambda i, j: (0, j)),
        ],
        out_specs=pl.BlockSpec((tm, tn), lambda i, j: (i, j)),
        out_shape=jax.ShapeDtypeStruct((m, n), BF16),
        scratch_shapes=[pltpu.VMEM((tm, d), BF16)],
        compiler_params=pltpu.CompilerParams(
            dimension_semantics=("arbitrary", "arbitrary"), vmem_limit_bytes=_vmem_limit(est)),
        name="in_proj",
    )(x2d, g.reshape(1, d), w.astype(BF16))


def _rotate(x, cos, sin_lo, sin_hi):
    half = ROT_DIM // 2
    return (x * cos + pltpu.roll(x, HEAD_DIM - half, 1) * sin_lo + pltpu.roll(x, half, 1) * sin_hi)


def _attn_kernel(invf_ref, q_ref, k_ref, v_ref, g_ref, o_ref,
                 cos_ref, slo_ref, shi_ref, qb_ref, kb_ref, vt_ref, s_ref, p_ref, *, seq):
    nb = seq // MOBA_BLOCK
    half = ROT_DIM // 2
    blk = MOBA_BLOCK
    nt_dims = (((1,), (1,)), ((), ()))

    @pl.when((pl.program_id(0) == 0) & (pl.program_id(1) == 0))
    def _():
        pos = lax.broadcasted_iota(jnp.int32, (seq, HEAD_DIM), 0).astype(F32)
        lane = lax.broadcasted_iota(jnp.int32, (seq, HEAD_DIM), 1)
        ang = pos * invf_ref[...]
        sin = jnp.sin(ang)
        cos_ref[...] = jnp.cos(ang)
        slo_ref[...] = jnp.where(lane < half, -sin, 0.0)
        shi_ref[...] = jnp.where(lane >= half, sin, 0.0)

    cos, slo, shi = cos_ref[...], slo_ref[...], shi_ref[...]
    qr = _rotate(q_ref[...].astype(F32), cos, slo, shi)
    kr = _rotate(k_ref[...].astype(F32), cos, slo, shi)
    qb_ref[...] = qr.astype(BF16)
    kb_ref[...] = kr.astype(BF16)
    vt_ref[...] = v_ref[...].T

    kmean = jnp.mean(kr.reshape(nb, blk, HEAD_DIM), axis=1)
    gate = lax.dot_general(kmean, qr, nt_dims, precision=lax.Precision.HIGHEST,
                           preferred_element_type=F32)

    c = HEAD_DIM ** -0.5 * 1.4426950408889634
    key_i = lax.broadcasted_iota(jnp.int32, (blk, blk), 0)
    qry_i = lax.broadcasted_iota(jnp.int32, (blk, blk), 1)
    causal = key_i <= qry_i
    blk_i = lax.broadcasted_iota(jnp.int32, (nb, blk), 0)
    gain = g_ref[...]

    for i in range(nb):
        qcols = slice(i * blk, (i + 1) * blk)
        klen = (i + 1) * blk
        qi = qb_ref[qcols, :]

        drop = None
        if i > MOBA_TOPK:
            g = gate[:, qcols]
            rank = jnp.zeros((nb, blk), jnp.int32)
            for jj in range(i):
                gj = jnp.broadcast_to(g[jj:jj + 1, :], (nb, blk))
                beats = (gj > g) | ((gj == g) & (jj < blk_i))
                rank = rank + beats.astype(jnp.int32)
            drop = jnp.where((rank < MOBA_TOPK) & (blk_i < i), 0.0, -jnp.inf)

        m_run = None
        for j in range(i + 1):
            krows = slice(j * blk, (j + 1) * blk)
            s = lax.dot_general(kb_ref[krows, :], qi, nt_dims, preferred_element_type=F32) * c
            if j == i:
                s = jnp.where(causal, s, -jnp.inf)
            elif drop is not None:
                s = s + drop[j:j + 1, :]
            s_ref[krows, :] = s
            m_run = s if m_run is None else jnp.maximum(m_run, s)
        m = jnp.max(m_run, axis=0, keepdims=True)

        l_run = None
        for j in range(i + 1):
            krows = slice(j * blk, (j + 1) * blk)
            p = jnp.exp2(s_ref[krows, :] - m)
            p_ref[krows, :] = p.astype(BF16)
            l_run = p if l_run is None else l_run + p
        denom = jnp.sum(l_run, axis=0, keepdims=True)

        o = jnp.dot(vt_ref[:, 0:klen], p_ref[0:klen, :], preferred_element_type=F32) / denom
        o = o * lax.rsqrt(jnp.mean(o * o, axis=0, keepdims=True) + EPS) * gain
        o_ref[qcols, :] = o.T.astype(o_ref.dtype)


def _moba_attn(u3, attn_g, n_heads):
    b, s, _ = u3.shape
    half = ROT_DIM // 2
    inv_freq = ROPE_THETA ** (-jnp.arange(half, dtype=F32) / half)
    invf = jnp.concatenate([inv_freq, inv_freq, jnp.zeros((HEAD_DIM - ROT_DIM,), F32)]).reshape(1, HEAD_DIM)
    blk = lambda off: pl.BlockSpec((None, s, HEAD_DIM), lambda bi, hi: (bi, 0, off + hi))
    est = (3 * 2 * s * HEAD_DIM * 2 + 2 * s * HEAD_DIM * 2 + 3 * s * HEAD_DIM * 4 + 3 * s * HEAD_DIM * 2
           + s * MOBA_BLOCK * (4 + 2) + 4 * s * HEAD_DIM * 4)
    return pl.pallas_call(
        functools.partial(_attn_kernel, seq=s),
        grid=(b, n_heads),
        in_specs=[
            pl.BlockSpec((1, HEAD_DIM), lambda bi, hi: (0, 0)),
            blk(0), blk(n_heads), blk(2 * n_heads),
            pl.BlockSpec((HEAD_DIM, 1), lambda bi, hi: (hi, 0)),
        ],
        out_specs=pl.BlockSpec((None, s, HEAD_DIM), lambda bi, hi: (bi, 0, hi)),
        out_shape=jax.ShapeDtypeStruct((b, s, n_heads * HEAD_DIM), BF16),
        scratch_shapes=[
            pltpu.VMEM((s, HEAD_DIM), F32), pltpu.VMEM((s, HEAD_DIM), F32), pltpu.VMEM((s, HEAD_DIM), F32),
            pltpu.VMEM((s, HEAD_DIM), BF16), pltpu.VMEM((s, HEAD_DIM), BF16),
            pltpu.VMEM((HEAD_DIM, s), BF16),
            pltpu.VMEM((s, MOBA_BLOCK), F32), pltpu.VMEM((s, MOBA_BLOCK), BF16),
        ],
        compiler_params=pltpu.CompilerParams(
            dimension_semantics=("arbitrary", "arbitrary"), vmem_limit_bytes=_vmem_limit(est)),
        name="moba_attn",
    )(invf, u3, u3, u3, attn_g.reshape(n_heads * HEAD_DIM, 1))


CONV_PHASES = V7X_SUBLANES
CONV_BACK = (CONV_KERNEL - 1) // V7X_SUBLANES * V7X_SUBLANES
CONV_LANES = 128


def _conv_kernel(cv_ref, cg_ref, w_ref, b_ref, lg_ref, lb_ref, o_ref, gp_ref, gs_ref, wb_ref,
                 *, t_rows, chunk):
    width = gp_ref.shape[1]
    sub = V7X_SUBLANES

    @pl.when((pl.program_id(0) == 0) & (pl.program_id(1) == 0))
    def _():
        for k in range(CONV_KERNEL):
            wb_ref[k] = jnp.broadcast_to(w_ref[k:k + 1, :], (sub, width))
        wb_ref[CONV_KERNEL] = jnp.broadcast_to(b_ref[...], (sub, width))

    @pl.when(pl.program_id(1) == 0)
    def _():
        gp_ref[0:CONV_HALO, :] = jnp.zeros((CONV_HALO, width), F32)

    @pl.when(pl.program_id(1) > 0)
    def _():
        gp_ref[0:CONV_HALO, :] = gp_ref[t_rows:t_rows + CONV_HALO, :]

    gp_ref[CONV_HALO:, :] = cv_ref[...].astype(F32) * jax.nn.sigmoid(cg_ref[...].astype(F32))

    n_shift = t_rows + CONV_BACK
    for r in range(1, CONV_PHASES):
        gs_ref[r - 1] = gp_ref[sub - r:sub - r + n_shift, :]

    lg, lb = lg_ref[...], lb_ref[...]
    groups = chunk // sub

    def conv_chunk(c, carry):
        t0 = pl.multiple_of(c * chunk, chunk)
        tiles = []
        for l0 in range(0, width, CONV_LANES):
            cols = slice(l0, l0 + CONV_LANES)
            acc = jnp.broadcast_to(wb_ref[CONV_KERNEL, :, cols][None], (groups, sub, CONV_LANES))
            for k in range(CONV_KERNEL):
                back, r = divmod(CONV_KERNEL - 1 - k, sub)
                if r == 0:
                    src = gp_ref[pl.ds(CONV_HALO - sub * back + t0, chunk), cols]
                else:
                    src = gs_ref[r - 1, pl.ds(CONV_HALO - sub - sub * back + t0, chunk), cols]
                acc = acc + src.reshape(groups, sub, CONV_LANES) * wb_ref[k, :, cols][None]
            tiles.append(acc.reshape(chunk, CONV_LANES))
        acc = jnp.concatenate(tiles, axis=1)
        mu = jnp.mean(acc, axis=-1, keepdims=True)
        xc = acc - mu
        var = jnp.mean(xc * xc, axis=-1, keepdims=True)
        y = xc * lax.rsqrt(var + EPS) * lg + lb
        o_ref[pl.ds(t0, chunk), :] = (y * jax.nn.sigmoid(y)).astype(o_ref.dtype)
        return carry

    lax.fori_loop(0, t_rows // chunk, conv_chunk, 0)


def _conv_mod(u3, col_off, width, w, bias, ln_g, ln_b, t_rows=256, chunk=64):
    b, s, _ = u3.shape
    cblk = col_off // width
    assert CONV_HALO >= CONV_KERNEL - 1 and CONV_HALO - V7X_SUBLANES >= CONV_BACK
    n_shift = t_rows + CONV_BACK
    tile_f32 = t_rows * width * 4
    est = (2 * 2 * t_rows * width * 2 + 2 * t_rows * width * 2 + (t_rows + CONV_HALO) * width * 4
           + (CONV_PHASES - 1) * n_shift * width * 4 + (CONV_KERNEL + 1) * V7X_SUBLANES * width * 4
           + 6 * tile_f32)
    vec = lambda a: a.reshape(1, width)
    vspec = pl.BlockSpec((1, width), lambda bi, ti: (0, 0))
    return pl.pallas_call(
        functools.partial(_conv_kernel, t_rows=t_rows, chunk=chunk),
        grid=(b, s // t_rows),
        in_specs=[
            pl.BlockSpec((None, t_rows, width), lambda bi, ti: (bi, ti, cblk)),
            pl.BlockSpec((None, t_rows, width), lambda bi, ti: (bi, ti, cblk + 1)),
            pl.BlockSpec((CONV_KERNEL, width), lambda bi, ti: (0, 0)),
            vspec, vspec, vspec,
        ],
        out_specs=pl.BlockSpec((None, t_rows, width), lambda bi, ti: (bi, ti, 0)),
        out_shape=jax.ShapeDtypeStruct((b, s, width), BF16),
        scratch_shapes=[
            pltpu.VMEM((t_rows + CONV_HALO, width), F32),
            pltpu.VMEM((CONV_PHASES - 1, n_shift, width), F32),
            pltpu.VMEM((CONV_KERNEL + 1, V7X_SUBLANES, width), F32),
        ],
        compiler_params=pltpu.CompilerParams(
            dimension_semantics=("arbitrary", "arbitrary"), vmem_limit_bytes=_vmem_limit(est)),
        name="conv_mod",
    )(u3, u3, w, vec(bias), vec(ln_g), vec(ln_b))


def _out_proj_kernel(x_ref, a_ref, c_ref, wa_ref, wc_ref, g_ref, x1_ref, hf_ref):
    y = (jnp.dot(a_ref[...], wa_ref[...], preferred_element_type=F32)
         + jnp.dot(c_ref[...], wc_ref[...], preferred_element_type=F32))
    x1 = x_ref[...] + y
    x1_ref[...] = x1
    hf_ref[...] = _rms(x1, g_ref[...]).astype(hf_ref.dtype)


def _out_proj(x2d, a2d, c2d, w_out, g, tm=512):
    m, d = x2d.shape
    ka, kc = a2d.shape[1], c2d.shape[1]
    assert ka == kc and ka + kc == w_out.shape[0]
    est = 2 * tm * d * 4 * 2 + 2 * tm * d * 2 + 2 * 2 * tm * ka * 2 + 2 * 2 * ka * d * 2
    return pl.pallas_call(
        _out_proj_kernel,
        grid=(m // tm,),
        in_specs=[
            pl.BlockSpec((tm, d), lambda i: (i, 0)),
            pl.BlockSpec((tm, ka), lambda i: (i, 0)),
            pl.BlockSpec((tm, kc), lambda i: (i, 0)),
            pl.BlockSpec((ka, d), lambda i: (0, 0)),
            pl.BlockSpec((kc, d), lambda i: (1, 0)),
            pl.BlockSpec((1, d), lambda i: (0, 0)),
        ],
        out_specs=[pl.BlockSpec((tm, d), lambda i: (i, 0)), pl.BlockSpec((tm, d), lambda i: (i, 0))],
        out_shape=[jax.ShapeDtypeStruct((m, d), F32), jax.ShapeDtypeStruct((m, d), BF16)],
        compiler_params=pltpu.CompilerParams(
            dimension_semantics=("arbitrary",), vmem_limit_bytes=_vmem_limit(est)),
        name="out_proj",
    )(x2d, a2d, c2d, w_out, w_out, g.reshape(1, d))


FFN_PIPE_DEPTH = 1
FFN_ROW_CHUNK = 64
FFN_LANE_CHUNK = 128
FFN_COL_CHUNK = 512


def _ffn_kernel(hf_ref, x1_ref, wua_ref, wub_ref, cwa_ref, cwb_ref, cba_ref, cbb_ref, wd_ref, gf_ref,
                o_ref, acc_ref, sa0_ref, sb0_ref, sa1_ref, sb1_ref, act_ref,
                carry_a_ref, carry_b_ref, *, tm, nf, n_items, tiles_per_seq, final_norm):
    t = pl.program_id(0)
    ta = jnp.minimum(t, n_items - 1)
    fa = ta % nf
    seq_start = ((ta // nf) % tiles_per_seq) == 0
    d = acc_ref.shape[1]

    @pl.when(t == 0)
    def _():
        for ref in (acc_ref, sa1_ref, sb1_ref, carry_a_ref, carry_b_ref):
            ref[...] = jnp.zeros(ref.shape, ref.dtype)

    def body(s_w, s_r):
        hf = hf_ref[...]

        def up_half(wu_ref, sw_ref, carry_ref):
            sw_ref[0:FFN_HALO, :] = jnp.where(seq_start, 0.0, carry_ref[fa])
            sw_ref[FFN_HALO:, :] = jnp.dot(hf, wu_ref[...], preferred_element_type=F32)
            carry_ref[fa] = sw_ref[tm:tm + FFN_HALO, :]

        up_half(wua_ref, s_w[0], carry_a_ref)

        def taps(cw_ref, cb_ref, cols):
            bc = lambda v: jnp.broadcast_to(v, (FFN_ROW_CHUNK, FFN_LANE_CHUNK))
            return [bc(cw_ref[k:k + 1, cols]) for k in range(FFN_CONV_KERNEL)], bc(cb_ref[:, cols])

        def conv(sr_ref, w, bias, r0, cols):
            y = bias
            for k in range(FFN_CONV_KERNEL):
                off = r0 + FFN_HALO - (FFN_CONV_KERNEL - 1) + k
                y = y + sr_ref[off:off + FFN_ROW_CHUNK, cols] * w[k]
            return y

        for c0 in range(0, act_ref.shape[1], FFN_LANE_CHUNK):
            cols = slice(c0, c0 + FFN_LANE_CHUNK)
            wa, ba = taps(cwa_ref, cba_ref, cols)
            wb, bb = taps(cwb_ref, cbb_ref, cols)
            for r0 in range(0, tm, FFN_ROW_CHUNK):
                ya = conv(s_r[0], wa, ba, r0, cols)
                yb = conv(s_r[1], wb, bb, r0, cols)
                act_ref[r0:r0 + FFN_ROW_CHUNK, cols] = (ya * jax.nn.sigmoid(ya) * yb).astype(BF16)

        act = act_ref[...]
        for c0 in range(0, d, FFN_COL_CHUNK):
            acc_ref[:, c0:c0 + FFN_COL_CHUNK] += jnp.dot(
                act, wd_ref[:, c0:c0 + FFN_COL_CHUNK], preferred_element_type=F32)

        up_half(wub_ref, s_w[1], carry_b_ref)

    @pl.when(t % 2 == 0)
    def _():
        body((sa0_ref, sb0_ref), (sa1_ref, sb1_ref))

    @pl.when(t % 2 == 1)
    def _():
        body((sa1_ref, sb1_ref), (sa0_ref, sb0_ref))

    @pl.when(t == 0)
    def _():
        acc_ref[...] = jnp.zeros(acc_ref.shape, F32)

    @pl.when((t >= FFN_PIPE_DEPTH) & ((t - FFN_PIPE_DEPTH) % nf == nf - 1))
    def _():
        x2 = x1_ref[...] + acc_ref[...]
        o_ref[...] = _rms(x2, gf_ref[...]) if final_norm else x2
        acc_ref[...] = jnp.zeros(acc_ref.shape, F32)


def _conv_ffn(hf, x1, w_up, cw, cb, w_down, g_final, seq, final_norm, tm=512, tf=512):
    m, d = hf.shape
    dff = w_down.shape[0]
    nf = dff // tf
    n_items = (m // tm) * nf
    s_bytes = (tm + FFN_HALO) * tf * 4
    est = (2 * tm * d * 2 + 2 * tm * d * 4 * 2 + tm * d * 4 + 2 * 2 * d * tf * 2 + 2 * tf * d * 2
           + 4 * s_bytes + tm * tf * 2 + 2 * nf * FFN_HALO * tf * 4 + 4 * tm * tf * 4)

    item = lambda t, lag: jnp.clip(t - lag, 0, n_items - 1)
    row_a = lambda t: (item(t, 0) // nf, 0)
    row_c = lambda t: (item(t, FFN_PIPE_DEPTH) // nf, 0)
    vspec = lambda r, half: pl.BlockSpec((r, tf), lambda t: (0, half * nf + item(t, FFN_PIPE_DEPTH) % nf))
    cb2 = cb.reshape(1, 2 * dff)
    w_up_bf = w_up.astype(BF16)
    s_scratch = pltpu.VMEM((tm + FFN_HALO, tf), F32)
    return pl.pallas_call(
        functools.partial(_ffn_kernel, tm=tm, nf=nf, n_items=n_items, tiles_per_seq=seq // tm,
                          final_norm=final_norm),
        grid=(n_items + FFN_PIPE_DEPTH,),
        in_specs=[
            pl.BlockSpec((tm, d), row_a),
            pl.BlockSpec((tm, d), row_c),
            pl.BlockSpec((d, tf), lambda t: (0, item(t, 0) % nf)),
            pl.BlockSpec((d, tf), lambda t: (0, nf + item(t, 0) % nf)),
            vspec(FFN_CONV_KERNEL, 0), vspec(FFN_CONV_KERNEL, 1), vspec(1, 0), vspec(1, 1),
            pl.BlockSpec((tf, d), lambda t: (item(t, FFN_PIPE_DEPTH) % nf, 0)),
            pl.BlockSpec((1, d), lambda t: (0, 0)),
        ],
        out_specs=pl.BlockSpec((tm, d), row_c),
        out_shape=jax.ShapeDtypeStruct((m, d), F32),
        scratch_shapes=[
            pltpu.VMEM((tm, d), F32),
            s_scratch, s_scratch, s_scratch, s_scratch,
            pltpu.VMEM((tm, tf), BF16),
            pltpu.VMEM((nf, FFN_HALO, tf), F32),
            pltpu.VMEM((nf, FFN_HALO, tf), F32),
        ],
        compiler_params=pltpu.CompilerParams(
            dimension_semantics=("arbitrary",), vmem_limit_bytes=_vmem_limit(est)),
        name="conv_ffn",
    )(hf, x1, w_up_bf, w_up_bf, cw, cw, cb2, cb2, w_down.astype(BF16), g_final.reshape(1, d))


def kernel(x, norm_mix_g, w_in, attn_out_g, conv_dw_w, conv_dw_b, conv_ln_g, conv_ln_b,
           w_out, norm_ffn_g, w_up, ffn_dw_w, ffn_dw_b, w_down, norm_final_g):
    b, s, d = x.shape
    depth = w_in.shape[0]
    attn_width = attn_out_g.shape[1]
    conv_width = conv_dw_b.shape[1]
    n_heads = attn_width // HEAD_DIM
    assert s % MOBA_BLOCK == 0 and w_in.shape[2] == 3 * attn_width + 2 * conv_width

    xf = x.reshape(b * s, d)
    for l in range(depth):
        u = _in_proj(xf, norm_mix_g[l], w_in[l])
        u3 = u.reshape(b, s, -1)
        a = _moba_attn(u3, attn_out_g[l], n_heads)
        c = _conv_mod(u3, 3 * attn_width, conv_width, conv_dw_w[l], conv_dw_b[l], conv_ln_g[l], conv_ln_b[l])
        x1, hf = _out_proj(xf, a.reshape(b * s, attn_width), c.reshape(b * s, conv_width),
                           w_out[l].astype(BF16), norm_ffn_g[l])
        xf = _conv_ffn(hf, x1, w_up[l], ffn_dw_w[l], ffn_dw_b[l], w_down[l],
                       norm_final_g, s, final_norm=(l == depth - 1))
    return xf.reshape(b, s, d)
```

```python
import functools

import jax
import jax.numpy as jnp
from jax import lax
from jax.experimental import pallas as pl
from jax.experimental.pallas import tpu as pltpu

F32 = jnp.float32
BF16 = jnp.bfloat16

HEAD_DIM = 128
ROT_DIM = HEAD_DIM // 4
ROPE_THETA = 500000.0
MOBA_BLOCK = 256
MOBA_TOPK = 3
CONV_KERNEL = 31
FFN_CONV_KERNEL = 3
EPS = 1e-5

V7X_SUBLANES = 8
V7X_VMEM_BYTES = 64 * 1024 * 1024

CONV_HALO = 32
FFN_HALO = V7X_SUBLANES


def _vmem_limit(nbytes):
    return int(min(nbytes * 1.5 + (8 << 20), V7X_VMEM_BYTES - (4 << 20)))


def _rms(x, g):
    return x * lax.rsqrt(jnp.mean(x * x, axis=-1, keepdims=True) + EPS) * g


def _in_proj_kernel(x_ref, g_ref, w_ref, o_ref, hn_ref):
    @pl.when(pl.program_id(1) == 0)
    def _():
        hn_ref[...] = _rms(x_ref[...], g_ref[...]).astype(BF16)

    o_ref[...] = jnp.dot(hn_ref[...], w_ref[...], preferred_element_type=F32).astype(o_ref.dtype)


def _in_proj(x2d, g, w, tm=1024, tn=1024):
    m, d = x2d.shape
    n = w.shape[1]
    est = 2 * tm * d * 4 + tm * d * 2 + 2 * d * tn * 2 + 2 * tm * tn * 2
    return pl.pallas_call(
        _in_proj_kernel,
        grid=(m // tm, n // tn),
        in_specs=[
            pl.BlockSpec((tm, d), lambda i, j: (i, 0)),
            pl.BlockSpec((1, d), lambda i, j: (0, 0)),
            pl.BlockSpec((d, tn), lambda i, j: (0, j)),
        ],
        out_specs=pl.BlockSpec((tm, tn), lambda i, j: (i, j)),
        out_shape=jax.ShapeDtypeStruct((m, n), BF16),
        scratch_shapes=[pltpu.VMEM((tm, d), BF16)],
        compiler_params=pltpu.CompilerParams(
            dimension_semantics=("arbitrary", "arbitrary"), vmem_limit_bytes=_vmem_limit(est)),
        name="in_proj",
    )(x2d, g.reshape(1, d), w.astype(BF16))


def _rotate(x, cos, sin_lo, sin_hi):
    half = ROT_DIM // 2
    return (x * cos + pltpu.roll(x, HEAD_DIM - half, 1) * sin_lo + pltpu.roll(x, half, 1) * sin_hi)


def _attn_kernel(invf_ref, q_ref, k_ref, v_ref, g_ref, o_ref,
                 cos_ref, slo_ref, shi_ref, qb_ref, kb_ref, vt_ref, s_ref, p_ref, *, seq):
    nb = seq // MOBA_BLOCK
    half = ROT_DIM // 2
    blk = MOBA_BLOCK
    nt_dims = (((1,), (1,)), ((), ()))

    @pl.when((pl.program_id(0) == 0) & (pl.program_id(1) == 0))
    def _():
        pos = lax.broadcasted_iota(jnp.int32, (seq, HEAD_DIM), 0).astype(F32)
        lane = lax.broadcasted_iota(jnp.int32, (seq, HEAD_DIM), 1)
        ang = pos * invf_ref[...]
        sin = jnp.sin(ang)
        cos_ref[...] = jnp.cos(ang)
        slo_ref[...] = jnp.where(lane < half, -sin, 0.0)
        shi_ref[...] = jnp.where(lane >= half, sin, 0.0)

    cos, slo, shi = cos_ref[...], slo_ref[...], shi_ref[...]
    qr = _rotate(q_ref[...].astype(F32), cos, slo, shi)
    kr = _rotate(k_ref[...].astype(F32), cos, slo, shi)
    qb_ref[...] = qr.astype(BF16)
    kb_ref[...] = kr.astype(BF16)
    vt_ref[...] = v_ref[...].T

    kmean = jnp.mean(kr.reshape(nb, blk, HEAD_DIM), axis=1)
    gate = lax.dot_general(kmean, qr, nt_dims, precision=lax.Precision.HIGHEST,
                           preferred_element_type=F32)

    c = HEAD_DIM ** -0.5 * 1.4426950408889634
    key_i = lax.broadcasted_iota(jnp.int32, (blk, blk), 0)
    qry_i = lax.broadcasted_iota(jnp.int32, (blk, blk), 1)
    causal = key_i <= qry_i
    blk_i = lax.broadcasted_iota(jnp.int32, (nb, blk), 0)
    gain = g_ref[...]

    for i in range(nb):
        qcols = slice(i * blk, (i + 1) * blk)
        klen = (i + 1) * blk
        qi = qb_ref[qcols, :]

        drop = None
        if i > MOBA_TOPK:
            g = gate[:, qcols]
            rank = jnp.zeros((nb, blk), jnp.int32)
            for jj in range(i):
                gj = jnp.broadcast_to(g[jj:jj + 1, :], (nb, blk))
                beats = (gj > g) | ((gj == g) & (jj < blk_i))
                rank = rank + beats.astype(jnp.int32)
            drop = jnp.where((rank < MOBA_TOPK) & (blk_i < i), 0.0, -jnp.inf)

        m_run = None
        for j in range(i + 1):
            krows = slice(j * blk, (j + 1) * blk)
            s = lax.dot_general(kb_ref[krows, :], qi, nt_dims, preferred_element_type=F32) * c
            if j == i:
                s = jnp.where(causal, s, -jnp.inf)
            elif drop is not None:
                s = s + drop[j:j + 1, :]
            s_ref[krows, :] = s
            m_run = s if m_run is None else jnp.maximum(m_run, s)
        m = jnp.max(m_run, axis=0, keepdims=True)

        l_run = None
        for j in range(i + 1):
            krows = slice(j * blk, (j + 1) * blk)
            p = jnp.exp2(s_ref[krows, :] - m)
            p_ref[krows, :] = p.astype(BF16)
            l_run = p if l_run is None else l_run + p
        denom = jnp.sum(l_run, axis=0, keepdims=True)

        o = jnp.dot(vt_ref[:, 0:klen], p_ref[0:klen, :], preferred_element_type=F32) / denom
        o = o * lax.rsqrt(jnp.mean(o * o, axis=0, keepdims=True) + EPS) * gain
        o_ref[qcols, :] = o.T.astype(o_ref.dtype)


def _moba_attn(u3, attn_g, n_heads):
    b, s, _ = u3.shape
    half = ROT_DIM // 2
    inv_freq = ROPE_THETA ** (-jnp.arange(half, dtype=F32) / half)
    invf = jnp.concatenate([inv_freq, inv_freq, jnp.zeros((HEAD_DIM - ROT_DIM,), F32)]).reshape(1, HEAD_DIM)
    blk = lambda off: pl.BlockSpec((None, s, HEAD_DIM), lambda bi, hi: (bi, 0, off + hi))
    est = (3 * 2 * s * HEAD_DIM * 2 + 2 * s * HEAD_DIM * 2 + 3 * s * HEAD_DIM * 4 + 3 * s * HEAD_DIM * 2
           + s * MOBA_BLOCK * (4 + 2) + 4 * s * HEAD_DIM * 4)
    return pl.pallas_call(
        functools.partial(_attn_kernel, seq=s),
        grid=(b, n_heads),
        in_specs=[
            pl.BlockSpec((1, HEAD_DIM), lambda bi, hi: (0, 0)),
            blk(0), blk(n_heads), blk(2 * n_heads),
            pl.BlockSpec((HEAD_DIM, 1), lambda bi, hi: (hi, 0)),
        ],
        out_specs=pl.BlockSpec((None, s, HEAD_DIM), lambda bi, hi: (bi, 0, hi)),
        out_shape=jax.ShapeDtypeStruct((b, s, n_heads * HEAD_DIM), BF16),
        scratch_shapes=[
            pltpu.VMEM((s, HEAD_DIM), F32), pltpu.VMEM((s, HEAD_DIM), F32), pltpu.VMEM((s, HEAD_DIM), F32),
            pltpu.VMEM((s, HEAD_DIM), BF16), pltpu.VMEM((s, HEAD_DIM), BF16),
            pltpu.VMEM((HEAD_DIM, s), BF16),
            pltpu.VMEM((s, MOBA_BLOCK), F32), pltpu.VMEM((s, MOBA_BLOCK), BF16),
        ],
        compiler_params=pltpu.CompilerParams(
            dimension_semantics=("arbitrary", "arbitrary"), vmem_limit_bytes=_vmem_limit(est)),
        name="moba_attn",
    )(invf, u3, u3, u3, attn_g.reshape(n_heads * HEAD_DIM, 1))


CONV_PHASES = V7X_SUBLANES
CONV_BACK = (CONV_KERNEL - 1) // V7X_SUBLANES * V7X_SUBLANES
CONV_LANES = 128


def _conv_kernel(cv_ref, cg_ref, w_ref, b_ref, lg_ref, lb_ref, o_ref, gp_ref, gs_ref, wb_ref,
                 *, t_rows, chunk):
    width = gp_ref.shape[1]
    sub = V7X_SUBLANES

    @pl.when((pl.program_id(0) == 0) & (pl.program_id(1) == 0))
    def _():
        for k in range(CONV_KERNEL):
            wb_ref[k] = jnp.broadcast_to(w_ref[k:k + 1, :], (sub, width))
        wb_ref[CONV_KERNEL] = jnp.broadcast_to(b_ref[...], (sub, width))

    @pl.when(pl.program_id(1) == 0)
    def _():
        gp_ref[0:CONV_HALO, :] = jnp.zeros((CONV_HALO, width), F32)

    @pl.when(pl.program_id(1) > 0)
    def _():
        gp_ref[0:CONV_HALO, :] = gp_ref[t_rows:t_rows + CONV_HALO, :]

    gp_ref[CONV_HALO:, :] = cv_ref[...].astype(F32) * jax.nn.sigmoid(cg_ref[...].astype(F32))

    n_shift = t_rows + CONV_BACK
    for r in range(1, CONV_PHASES):
        gs_ref[r - 1] = gp_ref[sub - r:sub - r + n_shift, :]

    lg, lb = lg_ref[...], lb_ref[...]
    groups = chunk // sub

    def conv_chunk(c, carry):
        t0 = pl.multiple_of(c * chunk, chunk)
        tiles = []
        for l0 in range(0, width, CONV_LANES):
            cols = slice(l0, l0 + CONV_LANES)
            acc = jnp.broadcast_to(wb_ref[CONV_KERNEL, :, cols][None], (groups, sub, CONV_LANES))
            for k in range(CONV_KERNEL):
                back, r = divmod(CONV_KERNEL - 1 - k, sub)
                if r == 0:
                    src = gp_ref[pl.ds(CONV_HALO - sub * back + t0, chunk), cols]
                else:
                    src = gs_ref[r - 1, pl.ds(CONV_HALO - sub - sub * back + t0, chunk), cols]
                acc = acc + src.reshape(groups, sub, CONV_LANES) * wb_ref[k, :, cols][None]
            tiles.append(acc.reshape(chunk, CONV_LANES))
        acc = jnp.concatenate(tiles, axis=1)
        mu = jnp.mean(acc, axis=-1, keepdims=True)
        xc = acc - mu
        var = jnp.mean(xc * xc, axis=-1, keepdims=True)
        y = xc * lax.rsqrt(var + EPS) * lg + lb
        o_ref[pl.ds(t0, chunk), :] = (y * jax.nn.sigmoid(y)).astype(o_ref.dtype)
        return carry

    lax.fori_loop(0, t_rows // chunk, conv_chunk, 0)


def _conv_mod(u3, col_off, width, w, bias, ln_g, ln_b, t_rows=256, chunk=64):
    b, s, _ = u3.shape
    cblk = col_off // width
    assert CONV_HALO >= CONV_KERNEL - 1 and CONV_HALO - V7X_SUBLANES >= CONV_BACK
    n_shift = t_rows + CONV_BACK
    tile_f32 = t_rows * width * 4
    est = (2 * 2 * t_rows * width * 2 + 2 * t_rows * width * 2 + (t_rows + CONV_HALO) * width * 4
           + (CONV_PHASES - 1) * n_shift * width * 4 + (CONV_KERNEL + 1) * V7X_SUBLANES * width * 4
           + 6 * tile_f32)
    vec = lambda a: a.reshape(1, width)
    vspec = pl.BlockSpec((1, width), lambda bi, ti: (0, 0))
    return pl.pallas_call(
        functools.partial(_conv_kernel, t_rows=t_rows, chunk=chunk),
        grid=(b, s // t_rows),
        in_specs=[
            pl.BlockSpec((None, t_rows, width), lambda bi, ti: (bi, ti, cblk)),
            pl.BlockSpec((None, t_rows, width), lambda bi, ti: (bi, ti, cblk + 1)),
            pl.BlockSpec((CONV_KERNEL, width), lambda bi, ti: (0, 0)),
            vspec, vspec, vspec,
        ],
        out_specs=pl.BlockSpec((None, t_rows, width), lambda bi, ti: (bi, ti, 0)),
        out_shape=jax.ShapeDtypeStruct((b, s, width), BF16),
        scratch_shapes=[
            pltpu.VMEM((t_rows + CONV_HALO, width), F32),
            pltpu.VMEM((CONV_PHASES - 1, n_shift, width), F32),
            pltpu.VMEM((CONV_KERNEL + 1, V7X_SUBLANES, width), F32),
        ],
        compiler_params=pltpu.CompilerParams(
            dimension_semantics=("arbitrary", "arbitrary"), vmem_limit_bytes=_vmem_limit(est)),
        name="conv_mod",
    )(u3, u3, w, vec(bias), vec(ln_g), vec(ln_b))


def _out_proj_kernel(x_ref, a_ref, c_ref, wa_ref, wc_ref, g_ref, x1_ref, hf_ref):
    y = (jnp.dot(a_ref[...], wa_ref[...], preferred_element_type=F32)
         + jnp.dot(c_ref[...], wc_ref[...], preferred_element_type=F32))
    x1 = x_ref[...] + y
    x1_ref[...] = x1
    hf_ref[...] = _rms(x1, g_ref[...]).astype(hf_ref.dtype)


def _out_proj(x2d, a2d, c2d, w_out, g, tm=512):
    m, d = x2d.shape
    ka, kc = a2d.shape[1], c2d.shape[1]
    assert ka == kc and ka + kc == w_out.shape[0]
    est = 2 * tm * d * 4 * 2 + 2 * tm * d * 2 + 2 * 2 * tm * ka * 2 + 2 * 2 * ka * d * 2
    return pl.pallas_call(
        _out_proj_kernel,
        grid=(m // tm,),
        in_specs=[
            pl.BlockSpec((tm, d), lambda i: (i, 0)),
            pl.BlockSpec((tm, ka), lambda i: (i, 0)),
            pl.BlockSpec((tm, kc), lambda i: (i, 0)),
            pl.BlockSpec((ka, d), lambda i: (0, 0)),
            pl.BlockSpec((kc, d), lambda i: (1, 0)),
            pl.BlockSpec((1, d), lambda i: (0, 0)),
        ],
        out_specs=[pl.BlockSpec((tm, d), lambda i: (i, 0)), pl.BlockSpec((tm, d), lambda i: (i, 0))],
        out_shape=[jax.ShapeDtypeStruct((m, d), F32), jax.ShapeDtypeStruct((m, d), BF16)],
        compiler_params=pltpu.CompilerParams(
            dimension_semantics=("arbitrary",), vmem_limit_bytes=_vmem_limit(est)),
        name="out_proj",
    )(x2d, a2d, c2d, w_out, w_out, g.reshape(1, d))


FFN_PIPE_DEPTH = 1
FFN_ROW_CHUNK = 64
FFN_LANE_CHUNK = 128
FFN_COL_CHUNK = 512


def _ffn_kernel(hf_ref, x1_ref, wua_ref, wub_ref, cwa_ref, cwb_ref, cba_ref, cbb_ref, wd_ref, gf_ref,
                o_ref, acc_ref, sa0_ref, sb0_ref, sa1_ref, sb1_ref, act_ref,
                carry_a_ref, carry_b_ref, *, tm, nf, n_items, tiles_per_seq, final_norm):
    t = pl.program_id(0)
    ta = jnp.minimum(t, n_items - 1)
    fa = ta % nf
    seq_start = ((ta // nf) % tiles_per_seq) == 0
    d = o_ref.shape[1]
    sub = V7X_SUBLANES
    lane_tiles = lambda width: [(lt, slice(lt * FFN_LANE_CHUNK, (lt + 1) * FFN_LANE_CHUNK))
                                for lt in range(width // FFN_LANE_CHUNK)]

    @pl.when(t == 0)
    def _():
        for ref in (acc_ref, sa1_ref, sb1_ref, carry_a_ref, carry_b_ref):
            ref[...] = jnp.zeros(ref.shape, ref.dtype)

    def body(s_w, s_r):
        hf = hf_ref[...]

        def up_half(wu_ref, sw_ref, carry_ref):
            up = jnp.dot(hf, wu_ref[...], preferred_element_type=F32)
            halo = jnp.where(seq_start, 0.0, carry_ref[fa])
            for lt, cols in lane_tiles(up.shape[1]):
                sw_ref[lt, 0:FFN_HALO, :] = halo[:, cols]
                sw_ref[lt, FFN_HALO:, :] = up[:, cols]
            carry_ref[fa] = up[tm - FFN_HALO:, :]

        up_half(wua_ref, s_w[0], carry_a_ref)

        def taps(cw_ref, cb_ref, cols):
            bc = lambda v: jnp.broadcast_to(v, (sub, FFN_LANE_CHUNK))
            return [bc(cw_ref[k:k + 1, cols]) for k in range(FFN_CONV_KERNEL)], bc(cb_ref[:, cols])

        def conv(sr_ref, w, bias, r0, lt):
            back = FFN_CONV_KERNEL - 1
            v = [sr_ref[lt, pl.ds(FFN_HALO + r0 + m, sub, stride=sub), :] for m in range(-back, sub)]
            return [bias + sum(v[m + k] * w[k] for k in range(FFN_CONV_KERNEL)) for m in range(sub)]

        for lt, cols in lane_tiles(act_ref.shape[1]):
            wa, ba = taps(cwa_ref, cba_ref, cols)
            wb, bb = taps(cwb_ref, cbb_ref, cols)
            for r0 in range(0, tm, FFN_ROW_CHUNK):
                ya = jnp.concatenate(conv(s_r[0], wa, ba, r0, lt), axis=0)
                yb = jnp.concatenate(conv(s_r[1], wb, bb, r0, lt), axis=0)
                act_ref[r0:r0 + FFN_ROW_CHUNK, cols] = (ya * jax.nn.sigmoid(ya) * yb).astype(BF16)

        act = act_ref[...]
        for c0 in range(0, d, FFN_COL_CHUNK):
            part = jnp.dot(act, wd_ref[:, c0:c0 + FFN_COL_CHUNK], preferred_element_type=F32)
            for lt, cols in lane_tiles(FFN_COL_CHUNK):
                acc_ref[c0 // FFN_LANE_CHUNK + lt] += part[:, cols]

        up_half(wub_ref, s_w[1], carry_b_ref)

    @pl.when(t % 2 == 0)
    def _():
        body((sa0_ref, sb0_ref), (sa1_ref, sb1_ref))

    @pl.when(t % 2 == 1)
    def _():
        body((sa1_ref, sb1_ref), (sa0_ref, sb0_ref))

    @pl.when(t == 0)
    def _():
        acc_ref[...] = jnp.zeros(acc_ref.shape, F32)

    @pl.when((t >= FFN_PIPE_DEPTH) & ((t - FFN_PIPE_DEPTH) % nf == nf - 1))
    def _():
        for lt, cols in lane_tiles(d):
            for r0 in range(0, tm, FFN_ROW_CHUNK):
                for j in range(FFN_ROW_CHUNK // sub):
                    rows = slice(r0 + sub * j, r0 + sub * (j + 1))
                    o_ref[rows, cols] = x1_ref[rows, cols] + acc_ref[lt, pl.ds(r0 + j, sub, stride=sub), :]
        if final_norm:
            o_ref[...] = _rms(o_ref[...], gf_ref[...])
        acc_ref[...] = jnp.zeros(acc_ref.shape, F32)


def _conv_ffn(hf, x1, w_up, cw, cb, w_down, g_final, seq, final_norm, tm=512, tf=512):
    m, d = hf.shape
    dff = w_down.shape[0]
    nf = dff // tf
    n_items = (m // tm) * nf
    s_bytes = (tm + FFN_HALO) * tf * 4
    est = (2 * tm * d * 2 + 2 * tm * d * 4 * 2 + tm * d * 4 + 2 * 2 * d * tf * 2 + 2 * tf * d * 2
           + 4 * s_bytes + tm * tf * 2 + 2 * nf * FFN_HALO * tf * 4 + 4 * tm * tf * 4)

    item = lambda t, lag: jnp.clip(t - lag, 0, n_items - 1)
    row_a = lambda t: (item(t, 0) // nf, 0)
    row_c = lambda t: (item(t, FFN_PIPE_DEPTH) // nf, 0)
    vspec = lambda r, half: pl.BlockSpec((r, tf), lambda t: (0, half * nf + item(t, FFN_PIPE_DEPTH) % nf))
    cb2 = cb.reshape(1, 2 * dff)
    w_up_bf = w_up.astype(BF16)
    assert FFN_ROW_CHUNK == V7X_SUBLANES * V7X_SUBLANES and tm % FFN_ROW_CHUNK == 0
    s_scratch = pltpu.VMEM((tf // FFN_LANE_CHUNK, tm + FFN_HALO, FFN_LANE_CHUNK), F32)
    return pl.pallas_call(
        functools.partial(_ffn_kernel, tm=tm, nf=nf, n_items=n_items, tiles_per_seq=seq // tm,
                          final_norm=final_norm),
        grid=(n_items + FFN_PIPE_DEPTH,),
        in_specs=[
            pl.BlockSpec((tm, d), row_a),
            pl.BlockSpec((tm, d), row_c),
            pl.BlockSpec((d, tf), lambda t: (0, item(t, 0) % nf)),
            pl.BlockSpec((d, tf), lambda t: (0, nf + item(t, 0) % nf)),
            vspec(FFN_CONV_KERNEL, 0), vspec(FFN_CONV_KERNEL, 1), vspec(1, 0), vspec(1, 1),
            pl.BlockSpec((tf, d), lambda t: (item(t, FFN_PIPE_DEPTH) % nf, 0)),
            pl.BlockSpec((1, d), lambda t: (0, 0)),
        ],
        out_specs=pl.BlockSpec((tm, d), row_c),
        out_shape=jax.ShapeDtypeStruct((m, d), F32),
        scratch_shapes=[
            pltpu.VMEM((d // FFN_LANE_CHUNK, tm, FFN_LANE_CHUNK), F32),
            s_scratch, s_scratch, s_scratch, s_scratch,
            pltpu.VMEM((tm, tf), BF16),
            pltpu.VMEM((nf, FFN_HALO, tf), F32),
            pltpu.VMEM((nf, FFN_HALO, tf), F32),
        ],
        compiler_params=pltpu.CompilerParams(
            dimension_semantics=("arbitrary",), vmem_limit_bytes=_vmem_limit(est)),
        name="conv_ffn",
    )(hf, x1, w_up_bf, w_up_bf, cw, cw, cb2, cb2, w_down.astype(BF16), g_final.reshape(1, d))


def kernel(x, norm_mix_g, w_in, attn_out_g, conv_dw_w, conv_dw_b, conv_ln_g, conv_ln_b,
           w_out, norm_ffn_g, w_up, ffn_dw_w, ffn_dw_b, w_down, norm_final_g):
    b, s, d = x.shape
    depth = w_in.shape[0]
    attn_width = attn_out_g.shape[1]
    conv_width = conv_dw_b.shape[1]
    n_heads = attn_width // HEAD_DIM
    assert s % MOBA_BLOCK == 0 and w_in.shape[2] == 3 * attn_width + 2 * conv_width

    xf = x.reshape(b * s, d)
    for l in range(depth):
        u = _in_proj(xf, norm_mix_g[l], w_in[l])
        u3 = u.reshape(b, s, -1)
        a = _moba_attn(u3, attn_out_g[l], n_heads)
        c = _conv_mod(u3, 3 * attn_width, conv_width, conv_dw_w[l], conv_dw_b[l], conv_ln_g[l], conv_ln_b[l])
        x1, hf = _out_proj(xf, a.reshape(b * s, attn_width), c.reshape(b * s, conv_width),
                           w_out[l].astype(BF16), norm_ffn_g[l])
        xf = _conv_ffn(hf, x1, w_up[l], ffn_dw_w[l], ffn_dw_b[l], w_down[l],
                       norm_final_g, s, final_norm=(l == depth - 1))
    return xf.reshape(b, s, d)
```

```python
import functools

import jax
import jax.numpy as jnp
from jax import lax
from jax.experimental import pallas as pl
from jax.experimental.pallas import tpu as pltpu

F32 = jnp.float32
BF16 = jnp.bfloat16

HEAD_DIM = 128
ROT_DIM = HEAD_DIM // 4
ROPE_THETA = 500000.0
MOBA_BLOCK = 256
MOBA_TOPK = 3
CONV_KERNEL = 31
FFN_CONV_KERNEL = 3
EPS = 1e-5

V7X_SUBLANES = 8
V7X_VMEM_BYTES = 64 * 1024 * 1024

CONV_HALO = 32
FFN_HALO = V7X_SUBLANES


def _vmem_limit(nbytes):
    return int(min(nbytes * 1.5 + (8 << 20), V7X_VMEM_BYTES - (4 << 20)))


def _rms(x, g):
    return x * lax.rsqrt(jnp.mean(x * x, axis=-1, keepdims=True) + EPS) * g


def _in_proj_kernel(x_ref, g_ref, w_ref, o_ref, hn_ref):
    @pl.when(pl.program_id(1) == 0)
    def _():
        hn_ref[...] = _rms(x_ref[...], g_ref[...]).astype(BF16)

    o_ref[...] = jnp.dot(hn_ref[...], w_ref[...].astype(BF16), preferred_element_type=F32).astype(o_ref.dtype)


def _in_proj(x2d, g, w, tm=1024, tn=1024):
    m, d = x2d.shape
    n = w.shape[1]
    est = 2 * tm * d * 4 + tm * d * 2 + 2 * d * tn * 4 + d * tn * 2 + 2 * tm * tn * 2
    return pl.pallas_call(
        _in_proj_kernel,
        grid=(m // tm, n // tn),
        in_specs=[
            pl.BlockSpec((tm, d), lambda i, j: (i, 0)),
            pl.BlockSpec((1, d), lambda i, j: (0, 0)),
            pl.BlockSpec((d, tn), lambda i, j: (0, j)),
        ],
        out_specs=pl.BlockSpec((tm, tn), lambda i, j: (i, j)),
        out_shape=jax.ShapeDtypeStruct((m, n), BF16),
        scratch_shapes=[pltpu.VMEM((tm, d), BF16)],
        compiler_params=pltpu.CompilerParams(
            dimension_semantics=("arbitrary", "arbitrary"), vmem_limit_bytes=_vmem_limit(est)),
        name="in_proj",
    )(x2d, g.reshape(1, d), w)


def _rotate(x, cos, sin_lo, sin_hi):
    half = ROT_DIM // 2
    return (x * cos + pltpu.roll(x, HEAD_DIM - half, 1) * sin_lo + pltpu.roll(x, half, 1) * sin_hi)


def _attn_kernel(invf_ref, q_ref, k_ref, v_ref, g_ref, o_ref,
                 cos_ref, slo_ref, shi_ref, qb_ref, kb_ref, vt_ref, s_ref, p_ref, *, seq):
    nb = seq // MOBA_BLOCK
    half = ROT_DIM // 2
    blk = MOBA_BLOCK
    nt_dims = (((1,), (1,)), ((), ()))

    @pl.when((pl.program_id(0) == 0) & (pl.program_id(1) == 0))
    def _():
        pos = lax.broadcasted_iota(jnp.int32, (seq, HEAD_DIM), 0).astype(F32)
        lane = lax.broadcasted_iota(jnp.int32, (seq, HEAD_DIM), 1)
        ang = pos * invf_ref[...]
        sin = jnp.sin(ang)
        cos_ref[...] = jnp.cos(ang)
        slo_ref[...] = jnp.where(lane < half, -sin, 0.0)
        shi_ref[...] = jnp.where(lane >= half, sin, 0.0)

    cos, slo, shi = cos_ref[...], slo_ref[...], shi_ref[...]
    qr = _rotate(q_ref[...].astype(F32), cos, slo, shi)
    kr = _rotate(k_ref[...].astype(F32), cos, slo, shi)
    qb_ref[...] = qr.astype(BF16)
    kb_ref[...] = kr.astype(BF16)
    vt_ref[...] = v_ref[...].T

    kmean = jnp.mean(kr.reshape(nb, blk, HEAD_DIM), axis=1)
    gate = lax.dot_general(kmean, qr, nt_dims, precision=lax.Precision.HIGHEST,
                           preferred_element_type=F32)

    c = HEAD_DIM ** -0.5 * 1.4426950408889634
    key_i = lax.broadcasted_iota(jnp.int32, (blk, blk), 0)
    qry_i = lax.broadcasted_iota(jnp.int32, (blk, blk), 1)
    causal = key_i <= qry_i
    blk_i = lax.broadcasted_iota(jnp.int32, (nb, blk), 0)
    gain = g_ref[...]

    for i in range(nb):
        qcols = slice(i * blk, (i + 1) * blk)
        klen = (i + 1) * blk
        qi = qb_ref[qcols, :]

        drop = None
        if i > MOBA_TOPK:
            g = gate[:, qcols]
            rank = jnp.zeros((nb, blk), jnp.int32)
            for jj in range(i):
                gj = jnp.broadcast_to(g[jj:jj + 1, :], (nb, blk))
                beats = (gj > g) | ((gj == g) & (jj < blk_i))
                rank = rank + beats.astype(jnp.int32)
            drop = jnp.where((rank < MOBA_TOPK) & (blk_i < i), 0.0, -jnp.inf)

        m_run = None
        for j in range(i + 1):
            krows = slice(j * blk, (j + 1) * blk)
            s = lax.dot_general(kb_ref[krows, :], qi, nt_dims, preferred_element_type=F32) * c
            if j == i:
                s = jnp.where(causal, s, -jnp.inf)
            elif drop is not None:
                s = s + drop[j:j + 1, :]
            s_ref[krows, :] = s
            m_run = s if m_run is None else jnp.maximum(m_run, s)
        m = jnp.max(m_run, axis=0, keepdims=True)

        l_run = None
        for j in range(i + 1):
            krows = slice(j * blk, (j + 1) * blk)
            p = jnp.exp2(s_ref[krows, :] - m)
            p_ref[krows, :] = p.astype(BF16)
            l_run = p if l_run is None else l_run + p
        denom = jnp.sum(l_run, axis=0, keepdims=True)

        o = jnp.dot(vt_ref[:, 0:klen], p_ref[0:klen, :], preferred_element_type=F32) / denom
        o = o * lax.rsqrt(jnp.mean(o * o, axis=0, keepdims=True) + EPS) * gain
        o_ref[qcols, :] = o.T.astype(o_ref.dtype)


def _moba_attn(u3, attn_g, n_heads):
    b, s, _ = u3.shape
    half = ROT_DIM // 2
    inv_freq = ROPE_THETA ** (-jnp.arange(half, dtype=F32) / half)
    invf = jnp.concatenate([inv_freq, inv_freq, jnp.zeros((HEAD_DIM - ROT_DIM,), F32)]).reshape(1, HEAD_DIM)
    blk = lambda off: pl.BlockSpec((None, s, HEAD_DIM), lambda bi, hi: (bi, 0, off + hi))
    est = (3 * 2 * s * HEAD_DIM * 2 + 2 * s * HEAD_DIM * 2 + 3 * s * HEAD_DIM * 4 + 3 * s * HEAD_DIM * 2
           + s * MOBA_BLOCK * (4 + 2) + 4 * s * HEAD_DIM * 4)
    return pl.pallas_call(
        functools.partial(_attn_kernel, seq=s),
        grid=(b, n_heads),
        in_specs=[
            pl.BlockSpec((1, HEAD_DIM), lambda bi, hi: (0, 0)),
            blk(0), blk(n_heads), blk(2 * n_heads),
            pl.BlockSpec((HEAD_DIM, 1), lambda bi, hi: (hi, 0)),
        ],
        out_specs=pl.BlockSpec((None, s, HEAD_DIM), lambda bi, hi: (bi, 0, hi)),
        out_shape=jax.ShapeDtypeStruct((b, s, n_heads * HEAD_DIM), BF16),
        scratch_shapes=[
            pltpu.VMEM((s, HEAD_DIM), F32), pltpu.VMEM((s, HEAD_DIM), F32), pltpu.VMEM((s, HEAD_DIM), F32),
            pltpu.VMEM((s, HEAD_DIM), BF16), pltpu.VMEM((s, HEAD_DIM), BF16),
            pltpu.VMEM((HEAD_DIM, s), BF16),
            pltpu.VMEM((s, MOBA_BLOCK), F32), pltpu.VMEM((s, MOBA_BLOCK), BF16),
        ],
        compiler_params=pltpu.CompilerParams(
            dimension_semantics=("arbitrary", "arbitrary"), vmem_limit_bytes=_vmem_limit(est)),
        name="moba_attn",
    )(invf, u3, u3, u3, attn_g.reshape(n_heads * HEAD_DIM, 1))


CONV_LANES = 128
CONV_ROWS = V7X_SUBLANES * V7X_SUBLANES


def _conv_kernel(cv_ref, cg_ref, w_ref, b_ref, lg_ref, lb_ref, o_ref, gp_ref, y_ref, wb_ref, *, t_rows):
    n_lt = gp_ref.shape[0]
    width = n_lt * CONV_LANES
    sub = V7X_SUBLANES
    lane_tiles = [(lt, slice(lt * CONV_LANES, (lt + 1) * CONV_LANES)) for lt in range(n_lt)]

    @pl.when((pl.program_id(0) == 0) & (pl.program_id(1) == 0))
    def _():
        for k in range(CONV_KERNEL):
            wb_ref[k] = jnp.broadcast_to(w_ref[k:k + 1, :], (sub, width))
        wb_ref[CONV_KERNEL] = jnp.broadcast_to(b_ref[...], (sub, width))

    @pl.when(pl.program_id(1) == 0)
    def _():
        gp_ref[:, 0:CONV_HALO, :] = jnp.zeros((n_lt, CONV_HALO, CONV_LANES), F32)

    @pl.when(pl.program_id(1) > 0)
    def _():
        gp_ref[:, 0:CONV_HALO, :] = gp_ref[:, t_rows:t_rows + CONV_HALO, :]

    for lt, cols in lane_tiles:
        gp_ref[lt, CONV_HALO:, :] = (cv_ref[:, cols].astype(F32)
                                     * jax.nn.sigmoid(cg_ref[:, cols].astype(F32)))

    def conv_chunk(c, carry):
        r0 = pl.multiple_of(c * CONV_ROWS, CONV_ROWS)
        for lt, cols in lane_tiles:
            def strided(m):
                return gp_ref[lt, pl.ds(CONV_HALO + r0 + m, sub, stride=sub), :]

            acc = [wb_ref[CONV_KERNEL, :, cols]] * sub
            for q in range(CONV_KERNEL):
                wq = wb_ref[CONV_KERNEL - 1 - q, :, cols]
                acc = [acc[m] + strided(m - q) * wq for m in range(sub)]
            y_ref[lt, pl.ds(r0, CONV_ROWS), :] = jnp.concatenate(acc, axis=0)
        return carry

    lax.fori_loop(0, t_rows // CONV_ROWS, conv_chunk, 0)

    lg, lb = lg_ref[...], lb_ref[...]

    def norm_chunk(c, carry):
        r0 = pl.multiple_of(c * CONV_ROWS, CONV_ROWS)
        acc = jnp.concatenate(
            [jnp.concatenate([y_ref[lt, pl.ds(r0 + j, sub, stride=sub), :] for lt, _ in lane_tiles], axis=1)
             for j in range(CONV_ROWS // sub)], axis=0)
        mu = jnp.mean(acc, axis=-1, keepdims=True)
        xc = acc - mu
        var = jnp.mean(xc * xc, axis=-1, keepdims=True)
        y = xc * lax.rsqrt(var + EPS) * lg + lb
        o_ref[pl.ds(r0, CONV_ROWS), :] = (y * jax.nn.sigmoid(y)).astype(o_ref.dtype)
        return carry

    lax.fori_loop(0, t_rows // CONV_ROWS, norm_chunk, 0)


def _conv_mod(u3, col_off, width, w, bias, ln_g, ln_b, t_rows=256):
    b, s, _ = u3.shape
    cblk = col_off // width
    n_lt = width // CONV_LANES
    assert CONV_HALO >= CONV_KERNEL - 1 and t_rows % CONV_ROWS == 0
    tile_f32 = t_rows * width * 4
    est = (2 * 2 * t_rows * width * 2 + 2 * t_rows * width * 2 + (t_rows + CONV_HALO) * width * 4
           + tile_f32 + (CONV_KERNEL + 1) * V7X_SUBLANES * width * 4
           + 4 * tile_f32)
    vec = lambda a: a.reshape(1, width)
    vspec = pl.BlockSpec((1, width), lambda bi, ti: (0, 0))
    return pl.pallas_call(
        functools.partial(_conv_kernel, t_rows=t_rows),
        grid=(b, s // t_rows),
        in_specs=[
            pl.BlockSpec((None, t_rows, width), lambda bi, ti: (bi, ti, cblk)),
            pl.BlockSpec((None, t_rows, width), lambda bi, ti: (bi, ti, cblk + 1)),
            pl.BlockSpec((CONV_KERNEL, width), lambda bi, ti: (0, 0)),
            vspec, vspec, vspec,
        ],
        out_specs=pl.BlockSpec((None, t_rows, width), lambda bi, ti: (bi, ti, 0)),
        out_shape=jax.ShapeDtypeStruct((b, s, width), BF16),
        scratch_shapes=[
            pltpu.VMEM((n_lt, t_rows + CONV_HALO, CONV_LANES), F32),
            pltpu.VMEM((n_lt, t_rows, CONV_LANES), F32),
            pltpu.VMEM((CONV_KERNEL + 1, V7X_SUBLANES, width), F32),
        ],
        compiler_params=pltpu.CompilerParams(
            dimension_semantics=("arbitrary", "arbitrary"), vmem_limit_bytes=_vmem_limit(est)),
        name="conv_mod",
    )(u3, u3, w, vec(bias), vec(ln_g), vec(ln_b))


def _out_proj_kernel(x_ref, a_ref, c_ref, wa_ref, wc_ref, g_ref, x1_ref, hf_ref):
    y = (jnp.dot(a_ref[...], wa_ref[...], preferred_element_type=F32)
         + jnp.dot(c_ref[...], wc_ref[...], preferred_element_type=F32))
    x1 = x_ref[...] + y
    x1_ref[...] = x1
    hf_ref[...] = _rms(x1, g_ref[...]).astype(hf_ref.dtype)


def _out_proj(x2d, a2d, c2d, w_out, g, tm=512):
    m, d = x2d.shape
    ka, kc = a2d.shape[1], c2d.shape[1]
    assert ka == kc and ka + kc == w_out.shape[0]
    est = 2 * tm * d * 4 * 2 + 2 * tm * d * 2 + 2 * 2 * tm * ka * 2 + 2 * 2 * ka * d * 2
    return pl.pallas_call(
        _out_proj_kernel,
        grid=(m // tm,),
        in_specs=[
            pl.BlockSpec((tm, d), lambda i: (i, 0)),
            pl.BlockSpec((tm, ka), lambda i: (i, 0)),
            pl.BlockSpec((tm, kc), lambda i: (i, 0)),
            pl.BlockSpec((ka, d), lambda i: (0, 0)),
            pl.BlockSpec((kc, d), lambda i: (1, 0)),
            pl.BlockSpec((1, d), lambda i: (0, 0)),
        ],
        out_specs=[pl.BlockSpec((tm, d), lambda i: (i, 0)), pl.BlockSpec((tm, d), lambda i: (i, 0))],
        out_shape=[jax.ShapeDtypeStruct((m, d), F32), jax.ShapeDtypeStruct((m, d), BF16)],
        compiler_params=pltpu.CompilerParams(
            dimension_semantics=("arbitrary",), vmem_limit_bytes=_vmem_limit(est)),
        name="out_proj",
    )(x2d, a2d, c2d, w_out, w_out, g.reshape(1, d))


FFN_PIPE_DEPTH = 1
FFN_ROW_CHUNK = 64
FFN_LANE_CHUNK = 128
FFN_COL_CHUNK = 512


def _ffn_kernel(hf_ref, x1_ref, wua_ref, wub_ref, cwa_ref, cwb_ref, cba_ref, cbb_ref, wd_ref, gf_ref,
                o_ref, acc_ref, sa0_ref, sb0_ref, sa1_ref, sb1_ref, act_ref,
                carry_a_ref, carry_b_ref, *, tm, nf, n_items, tiles_per_seq, final_norm):
    t = pl.program_id(0)
    ta = jnp.minimum(t, n_items - 1)
    fa = ta % nf
    seq_start = ((ta // nf) % tiles_per_seq) == 0
    d = o_ref.shape[1]
    sub = V7X_SUBLANES
    lane_tiles = lambda width: [(lt, slice(lt * FFN_LANE_CHUNK, (lt + 1) * FFN_LANE_CHUNK))
                                for lt in range(width // FFN_LANE_CHUNK)]

    @pl.when(t == 0)
    def _():
        for ref in (acc_ref, sa1_ref, sb1_ref, carry_a_ref, carry_b_ref):
            ref[...] = jnp.zeros(ref.shape, ref.dtype)

    def body(s_w, s_r):
        hf = hf_ref[...]

        def up_half(wu_ref, sw_ref, carry_ref):
            up = jnp.dot(hf, wu_ref[...], preferred_element_type=F32)
            halo = jnp.where(seq_start, 0.0, carry_ref[fa])
            for lt, cols in lane_tiles(up.shape[1]):
                sw_ref[lt, 0:FFN_HALO, :] = halo[:, cols]
                sw_ref[lt, FFN_HALO:, :] = up[:, cols]
            carry_ref[fa] = up[tm - FFN_HALO:, :]

        up_half(wua_ref, s_w[0], carry_a_ref)

        def taps(cw_ref, cb_ref, cols):
            bc = lambda v: jnp.broadcast_to(v, (sub, FFN_LANE_CHUNK))
            return [bc(cw_ref[k:k + 1, cols]) for k in range(FFN_CONV_KERNEL)], bc(cb_ref[:, cols])

        def conv(sr_ref, w, bias, r0, lt):
            back = FFN_CONV_KERNEL - 1
            v = [sr_ref[lt, pl.ds(FFN_HALO + r0 + m, sub, stride=sub), :] for m in range(-back, sub)]
            return [bias + sum(v[m + k] * w[k] for k in range(FFN_CONV_KERNEL)) for m in range(sub)]

        for lt, cols in lane_tiles(act_ref.shape[1]):
            wa, ba = taps(cwa_ref, cba_ref, cols)
            wb, bb = taps(cwb_ref, cbb_ref, cols)
            for r0 in range(0, tm, FFN_ROW_CHUNK):
                ya = jnp.concatenate(conv(s_r[0], wa, ba, r0, lt), axis=0)
                yb = jnp.concatenate(conv(s_r[1], wb, bb, r0, lt), axis=0)
                act_ref[r0:r0 + FFN_ROW_CHUNK, cols] = (ya * jax.nn.sigmoid(ya) * yb).astype(BF16)

        act = act_ref[...]
        for c0 in range(0, d, FFN_COL_CHUNK):
            part = jnp.dot(act, wd_ref[:, c0:c0 + FFN_COL_CHUNK], preferred_element_type=F32)
            for lt, cols in lane_tiles(FFN_COL_CHUNK):
                acc_ref[c0 // FFN_LANE_CHUNK + lt] += part[:, cols]

        up_half(wub_ref, s_w[1], carry_b_ref)

    @pl.when(t % 2 == 0)
    def _():
        body((sa0_ref, sb0_ref), (sa1_ref, sb1_ref))

    @pl.when(t % 2 == 1)
    def _():
        body((sa1_ref, sb1_ref), (sa0_ref, sb0_ref))

    @pl.when(t == 0)
    def _():
        acc_ref[...] = jnp.zeros(acc_ref.shape, F32)

    @pl.when((t >= FFN_PIPE_DEPTH) & ((t - FFN_PIPE_DEPTH) % nf == nf - 1))
    def _():
        for lt, cols in lane_tiles(d):
            for r0 in range(0, tm, FFN_ROW_CHUNK):
                for j in range(FFN_ROW_CHUNK // sub):
                    rows = slice(r0 + sub * j, r0 + sub * (j + 1))
                    o_ref[rows, cols] = x1_ref[rows, cols] + acc_ref[lt, pl.ds(r0 + j, sub, stride=sub), :]
        if final_norm:
            o_ref[...] = _rms(o_ref[...], gf_ref[...])
        acc_ref[...] = jnp.zeros(acc_ref.shape, F32)


def _conv_ffn(hf, x1, w_up, cw, cb, w_down, g_final, seq, final_norm, tm=512, tf=512):
    m, d = hf.shape
    dff = w_down.shape[0]
    nf = dff // tf
    n_items = (m // tm) * nf
    s_bytes = (tm + FFN_HALO) * tf * 4
    est = (2 * tm * d * 2 + 2 * tm * d * 4 * 2 + tm * d * 4 + 2 * 2 * d * tf * 2 + 2 * tf * d * 2
           + 4 * s_bytes + tm * tf * 2 + 2 * nf * FFN_HALO * tf * 4 + 4 * tm * tf * 4)

    item = lambda t, lag: jnp.clip(t - lag, 0, n_items - 1)
    row_a = lambda t: (item(t, 0) // nf, 0)
    row_c = lambda t: (item(t, FFN_PIPE_DEPTH) // nf, 0)
    vspec = lambda r, half: pl.BlockSpec((r, tf), lambda t: (0, half * nf + item(t, FFN_PIPE_DEPTH) % nf))
    cb2 = cb.reshape(1, 2 * dff)
    w_up_bf = w_up.astype(BF16)
    assert FFN_ROW_CHUNK == V7X_SUBLANES * V7X_SUBLANES and tm % FFN_ROW_CHUNK == 0
    s_scratch = pltpu.VMEM((tf // FFN_LANE_CHUNK, tm + FFN_HALO, FFN_LANE_CHUNK), F32)
    return pl.pallas_call(
        functools.partial(_ffn_kernel, tm=tm, nf=nf, n_items=n_items, tiles_per_seq=seq // tm,
                          final_norm=final_norm),
        grid=(n_items + FFN_PIPE_DEPTH,),
        in_specs=[
            pl.BlockSpec((tm, d), row_a),
            pl.BlockSpec((tm, d), row_c),
            pl.BlockSpec((d, tf), lambda t: (0, item(t, 0) % nf)),
            pl.BlockSpec((d, tf), lambda t: (0, nf + item(t, 0) % nf)),
            vspec(FFN_CONV_KERNEL, 0), vspec(FFN_CONV_KERNEL, 1), vspec(1, 0), vspec(1, 1),
            pl.BlockSpec((tf, d), lambda t: (item(t, FFN_PIPE_DEPTH) % nf, 0)),
            pl.BlockSpec((1, d), lambda t: (0, 0)),
        ],
        out_specs=pl.BlockSpec((tm, d), row_c),
        out_shape=jax.ShapeDtypeStruct((m, d), F32),
        scratch_shapes=[
            pltpu.VMEM((d // FFN_LANE_CHUNK, tm, FFN_LANE_CHUNK), F32),
            s_scratch, s_scratch, s_scratch, s_scratch,
            pltpu.VMEM((tm, tf), BF16),
            pltpu.VMEM((nf, FFN_HALO, tf), F32),
            pltpu.VMEM((nf, FFN_HALO, tf), F32),
        ],
        compiler_params=pltpu.CompilerParams(
            dimension_semantics=("arbitrary",), vmem_limit_bytes=_vmem_limit(est)),
        name="conv_ffn",
    )(hf, x1, w_up_bf, w_up_bf, cw, cw, cb2, cb2, w_down.astype(BF16), g_final.reshape(1, d))


def kernel(x, norm_mix_g, w_in, attn_out_g, conv_dw_w, conv_dw_b, conv_ln_g, conv_ln_b,
           w_out, norm_ffn_g, w_up, ffn_dw_w, ffn_dw_b, w_down, norm_final_g):
    b, s, d = x.shape
    depth = w_in.shape[0]
    attn_width = attn_out_g.shape[1]
    conv_width = conv_dw_b.shape[1]
    n_heads = attn_width // HEAD_DIM
    assert s % MOBA_BLOCK == 0 and w_in.shape[2] == 3 * attn_width + 2 * conv_width

    xf = x.reshape(b * s, d)
    for l in range(depth):
        u = _in_proj(xf, norm_mix_g[l], w_in[l])
        u3 = u.reshape(b, s, -1)
        a = _moba_attn(u3, attn_out_g[l], n_heads)
        c = _conv_mod(u3, 3 * attn_width, conv_width, conv_dw_w[l], conv_dw_b[l], conv_ln_g[l], conv_ln_b[l])
        x1, hf = _out_proj(xf, a.reshape(b * s, attn_width), c.reshape(b * s, conv_width),
                           w_out[l].astype(BF16), norm_ffn_g[l])
        xf = _conv_ffn(hf, x1, w_up[l], ffn_dw_w[l], ffn_dw_b[l], w_down[l],
                       norm_final_g, s, final_norm=(l == depth - 1))
    return xf.reshape(b, s, d)
```

```python
import functools

import jax
import jax.numpy as jnp
from jax import lax
from jax.experimental import pallas as pl
from jax.experimental.pallas import tpu as pltpu

F32 = jnp.float32
BF16 = jnp.bfloat16

HEAD_DIM = 128
ROT_DIM = HEAD_DIM // 4
ROPE_THETA = 500000.0
MOBA_BLOCK = 256
MOBA_TOPK = 3
CONV_KERNEL = 31
FFN_CONV_KERNEL = 3
EPS = 1e-5

V7X_SUBLANES = 8
V7X_VMEM_BYTES = 64 * 1024 * 1024

CONV_HALO = 32
FFN_HALO = V7X_SUBLANES


def _vmem_limit(nbytes):
    return int(min(nbytes * 1.5 + (8 << 20), V7X_VMEM_BYTES - (4 << 20)))


def _rms(x, g):
    return x * lax.rsqrt(jnp.mean(x * x, axis=-1, keepdims=True) + EPS) * g


def _in_proj_kernel(x_ref, g_ref, w_ref, o_ref, hn_ref):
    @pl.when(pl.program_id(1) == 0)
    def _():
        hn_ref[...] = _rms(x_ref[...], g_ref[...]).astype(BF16)

    o_ref[...] = jnp.dot(hn_ref[...], w_ref[...].astype(BF16), preferred_element_type=F32).astype(o_ref.dtype)


def _in_proj(x2d, g, w, tm=1024, tn=1024):
    m, d = x2d.shape
    n = w.shape[1]
    est = 2 * tm * d * 4 + tm * d * 2 + 2 * d * tn * 4 + d * tn * 2 + 2 * tm * tn * 2
    return pl.pallas_call(
        _in_proj_kernel,
        grid=(m // tm, n // tn),
        in_specs=[
            pl.BlockSpec((tm, d), lambda i, j: (i, 0)),
            pl.BlockSpec((1, d), lambda i, j: (0, 0)),
            pl.BlockSpec((d, tn), lambda i, j: (0, j)),
        ],
        out_specs=pl.BlockSpec((tm, tn), lambda i, j: (i, j)),
        out_shape=jax.ShapeDtypeStruct((m, n), BF16),
        scratch_shapes=[pltpu.VMEM((tm, d), BF16)],
        compiler_params=pltpu.CompilerParams(
            dimension_semantics=("arbitrary", "arbitrary"), vmem_limit_bytes=_vmem_limit(est)),
        name="in_proj",
    )(x2d, g.reshape(1, d), w)


def _rotate(x, cos, sin_lo, sin_hi):
    half = ROT_DIM // 2
    return (x * cos + pltpu.roll(x, HEAD_DIM - half, 1) * sin_lo + pltpu.roll(x, half, 1) * sin_hi)


def _attn_kernel(invf_ref, q_ref, k_ref, v_ref, g_ref, o_ref,
                 cos_ref, slo_ref, shi_ref, qb_ref, kb_ref, vt_ref, s_ref, p_ref, *, seq):
    nb = seq // MOBA_BLOCK
    half = ROT_DIM // 2
    blk = MOBA_BLOCK
    nt_dims = (((1,), (1,)), ((), ()))

    @pl.when((pl.program_id(0) == 0) & (pl.program_id(1) == 0))
    def _():
        pos = lax.broadcasted_iota(jnp.int32, (seq, HEAD_DIM), 0).astype(F32)
        lane = lax.broadcasted_iota(jnp.int32, (seq, HEAD_DIM), 1)
        ang = pos * invf_ref[...]
        sin = jnp.sin(ang)
        cos_ref[...] = jnp.cos(ang)
        slo_ref[...] = jnp.where(lane < half, -sin, 0.0)
        shi_ref[...] = jnp.where(lane >= half, sin, 0.0)

    cos, slo, shi = cos_ref[...], slo_ref[...], shi_ref[...]
    qr = _rotate(q_ref[...].astype(F32), cos, slo, shi)
    kr = _rotate(k_ref[...].astype(F32), cos, slo, shi)
    qb_ref[...] = qr.astype(BF16)
    kb_ref[...] = kr.astype(BF16)
    vt_ref[...] = v_ref[...].T

    kmean = jnp.mean(kr.reshape(nb, blk, HEAD_DIM), axis=1)
    gate = lax.dot_general(kmean, qr, nt_dims, precision=lax.Precision.HIGHEST,
                           preferred_element_type=F32)

    c = HEAD_DIM ** -0.5 * 1.4426950408889634
    key_i = lax.broadcasted_iota(jnp.int32, (blk, blk), 0)
    qry_i = lax.broadcasted_iota(jnp.int32, (blk, blk), 1)
    causal = key_i <= qry_i
    blk_i = lax.broadcasted_iota(jnp.int32, (nb, blk), 0)
    gain = g_ref[...]

    for i in range(nb):
        qcols = slice(i * blk, (i + 1) * blk)
        klen = (i + 1) * blk
        qi = qb_ref[qcols, :]

        drop = None
        if i > MOBA_TOPK:
            g = gate[:, qcols]
            rank = jnp.zeros((nb, blk), jnp.int32)
            for jj in range(i):
                gj = jnp.broadcast_to(g[jj:jj + 1, :], (nb, blk))
                beats = (gj > g) | ((gj == g) & (jj < blk_i))
                rank = rank + beats.astype(jnp.int32)
            drop = jnp.where((rank < MOBA_TOPK) & (blk_i < i), 0.0, -jnp.inf)

        m_run = None
        for j in range(i + 1):
            krows = slice(j * blk, (j + 1) * blk)
            s = lax.dot_general(kb_ref[krows, :], qi, nt_dims, preferred_element_type=F32) * c
            if j == i:
                s = jnp.where(causal, s, -jnp.inf)
            elif drop is not None:
                s = s + drop[j:j + 1, :]
            s_ref[krows, :] = s
            m_run = s if m_run is None else jnp.maximum(m_run, s)
        m = jnp.max(m_run, axis=0, keepdims=True)

        l_run = None
        for j in range(i + 1):
            krows = slice(j * blk, (j + 1) * blk)
            p = jnp.exp2(s_ref[krows, :] - m)
            p_ref[krows, :] = p.astype(BF16)
            l_run = p if l_run is None else l_run + p
        denom = jnp.sum(l_run, axis=0, keepdims=True)

        o = jnp.dot(vt_ref[:, 0:klen], p_ref[0:klen, :], preferred_element_type=F32) / denom
        o = o * lax.rsqrt(jnp.mean(o * o, axis=0, keepdims=True) + EPS) * gain
        o_ref[qcols, :] = o.T.astype(o_ref.dtype)


def _moba_attn(u3, attn_g, n_heads):
    b, s, _ = u3.shape
    half = ROT_DIM // 2
    inv_freq = ROPE_THETA ** (-jnp.arange(half, dtype=F32) / half)
    invf = jnp.concatenate([inv_freq, inv_freq, jnp.zeros((HEAD_DIM - ROT_DIM,), F32)]).reshape(1, HEAD_DIM)
    blk = lambda off: pl.BlockSpec((None, s, HEAD_DIM), lambda bi, hi: (bi, 0, off + hi))
    est = (3 * 2 * s * HEAD_DIM * 2 + 2 * s * HEAD_DIM * 2 + 3 * s * HEAD_DIM * 4 + 3 * s * HEAD_DIM * 2
           + s * MOBA_BLOCK * (4 + 2) + 4 * s * HEAD_DIM * 4)
    return pl.pallas_call(
        functools.partial(_attn_kernel, seq=s),
        grid=(b, n_heads),
        in_specs=[
            pl.BlockSpec((1, HEAD_DIM), lambda bi, hi: (0, 0)),
            blk(0), blk(n_heads), blk(2 * n_heads),
            pl.BlockSpec((HEAD_DIM, 1), lambda bi, hi: (hi, 0)),
        ],
        out_specs=pl.BlockSpec((None, s, HEAD_DIM), lambda bi, hi: (bi, 0, hi)),
        out_shape=jax.ShapeDtypeStruct((b, s, n_heads * HEAD_DIM), BF16),
        scratch_shapes=[
            pltpu.VMEM((s, HEAD_DIM), F32), pltpu.VMEM((s, HEAD_DIM), F32), pltpu.VMEM((s, HEAD_DIM), F32),
            pltpu.VMEM((s, HEAD_DIM), BF16), pltpu.VMEM((s, HEAD_DIM), BF16),
            pltpu.VMEM((HEAD_DIM, s), BF16),
            pltpu.VMEM((s, MOBA_BLOCK), F32), pltpu.VMEM((s, MOBA_BLOCK), BF16),
        ],
        compiler_params=pltpu.CompilerParams(
            dimension_semantics=("arbitrary", "arbitrary"), vmem_limit_bytes=_vmem_limit(est)),
        name="moba_attn",
    )(invf, u3, u3, u3, attn_g.reshape(n_heads * HEAD_DIM, 1))


CONV_LANES = 128
CONV_ROWS = V7X_SUBLANES * V7X_SUBLANES


def _conv_kernel(cv_ref, cg_ref, w_ref, b_ref, lg_ref, lb_ref, o_ref, gp_ref, y_ref, wb_ref, *, t_rows):
    n_lt = gp_ref.shape[0]
    width = n_lt * CONV_LANES
    sub = V7X_SUBLANES
    lane_tiles = [(lt, slice(lt * CONV_LANES, (lt + 1) * CONV_LANES)) for lt in range(n_lt)]

    @pl.when((pl.program_id(0) == 0) & (pl.program_id(1) == 0))
    def _():
        for k in range(CONV_KERNEL):
            wb_ref[k] = jnp.broadcast_to(w_ref[k:k + 1, :], (sub, width))
        wb_ref[CONV_KERNEL] = jnp.broadcast_to(b_ref[...], (sub, width))

    @pl.when(pl.program_id(1) == 0)
    def _():
        gp_ref[:, 0:CONV_HALO, :] = jnp.zeros((n_lt, CONV_HALO, CONV_LANES), F32)

    @pl.when(pl.program_id(1) > 0)
    def _():
        gp_ref[:, 0:CONV_HALO, :] = gp_ref[:, t_rows:t_rows + CONV_HALO, :]

    for lt, cols in lane_tiles:
        gp_ref[lt, CONV_HALO:, :] = (cv_ref[:, cols].astype(F32)
                                     * jax.nn.sigmoid(cg_ref[:, cols].astype(F32)))

    def conv_chunk(c, carry):
        r0 = pl.multiple_of(c * CONV_ROWS, CONV_ROWS)
        for lt, cols in lane_tiles:
            def strided(m):
                return gp_ref[lt, pl.ds(CONV_HALO + r0 + m, sub, stride=sub), :]

            acc = [wb_ref[CONV_KERNEL, :, cols]] * sub
            for q in range(CONV_KERNEL):
                wq = wb_ref[CONV_KERNEL - 1 - q, :, cols]
                acc = [acc[m] + strided(m - q) * wq for m in range(sub)]
            y_ref[lt, pl.ds(r0, CONV_ROWS), :] = jnp.concatenate(acc, axis=0)
        return carry

    lax.fori_loop(0, t_rows // CONV_ROWS, conv_chunk, 0)

    lg, lb = lg_ref[...], lb_ref[...]

    def norm_chunk(c, carry):
        r0 = pl.multiple_of(c * CONV_ROWS, CONV_ROWS)
        acc = jnp.concatenate(
            [jnp.concatenate([y_ref[lt, pl.ds(r0 + j, sub, stride=sub), :] for lt, _ in lane_tiles], axis=1)
             for j in range(CONV_ROWS // sub)], axis=0)
        mu = jnp.mean(acc, axis=-1, keepdims=True)
        xc = acc - mu
        var = jnp.mean(xc * xc, axis=-1, keepdims=True)
        y = xc * lax.rsqrt(var + EPS) * lg + lb
        o_ref[pl.ds(r0, CONV_ROWS), :] = (y * jax.nn.sigmoid(y)).astype(o_ref.dtype)
        return carry

    lax.fori_loop(0, t_rows // CONV_ROWS, norm_chunk, 0)


def _conv_mod(u3, col_off, width, w, bias, ln_g, ln_b, t_rows=256):
    b, s, _ = u3.shape
    cblk = col_off // width
    n_lt = width // CONV_LANES
    assert CONV_HALO >= CONV_KERNEL - 1 and t_rows % CONV_ROWS == 0
    tile_f32 = t_rows * width * 4
    est = (2 * 2 * t_rows * width * 2 + 2 * t_rows * width * 2 + (t_rows + CONV_HALO) * width * 4
           + tile_f32 + (CONV_KERNEL + 1) * V7X_SUBLANES * width * 4
           + 4 * tile_f32)
    vec = lambda a: a.reshape(1, width)
    vspec = pl.BlockSpec((1, width), lambda bi, ti: (0, 0))
    return pl.pallas_call(
        functools.partial(_conv_kernel, t_rows=t_rows),
        grid=(b, s // t_rows),
        in_specs=[
            pl.BlockSpec((None, t_rows, width), lambda bi, ti: (bi, ti, cblk)),
            pl.BlockSpec((None, t_rows, width), lambda bi, ti: (bi, ti, cblk + 1)),
            pl.BlockSpec((CONV_KERNEL, width), lambda bi, ti: (0, 0)),
            vspec, vspec, vspec,
        ],
        out_specs=pl.BlockSpec((None, t_rows, width), lambda bi, ti: (bi, ti, 0)),
        out_shape=jax.ShapeDtypeStruct((b, s, width), BF16),
        scratch_shapes=[
            pltpu.VMEM((n_lt, t_rows + CONV_HALO, CONV_LANES), F32),
            pltpu.VMEM((n_lt, t_rows, CONV_LANES), F32),
            pltpu.VMEM((CONV_KERNEL + 1, V7X_SUBLANES, width), F32),
        ],
        compiler_params=pltpu.CompilerParams(
            dimension_semantics=("arbitrary", "arbitrary"), vmem_limit_bytes=_vmem_limit(est)),
        name="conv_mod",
    )(u3, u3, w, vec(bias), vec(ln_g), vec(ln_b))


def _out_proj_kernel(x_ref, a_ref, c_ref, wa_ref, wc_ref, g_ref, x1_ref, hf_ref):
    y = (jnp.dot(a_ref[...], wa_ref[...], preferred_element_type=F32)
         + jnp.dot(c_ref[...], wc_ref[...], preferred_element_type=F32))
    x1 = x_ref[...] + y
    x1_ref[...] = x1
    hf_ref[...] = _rms(x1, g_ref[...]).astype(hf_ref.dtype)


def _out_proj(x2d, a2d, c2d, w_out, g, tm=512):
    m, d = x2d.shape
    ka, kc = a2d.shape[1], c2d.shape[1]
    assert ka == kc and ka + kc == w_out.shape[0]
    est = 2 * tm * d * 4 * 2 + 2 * tm * d * 2 + 2 * 2 * tm * ka * 2 + 2 * 2 * ka * d * 2
    return pl.pallas_call(
        _out_proj_kernel,
        grid=(m // tm,),
        in_specs=[
            pl.BlockSpec((tm, d), lambda i: (i, 0)),
            pl.BlockSpec((tm, ka), lambda i: (i, 0)),
            pl.BlockSpec((tm, kc), lambda i: (i, 0)),
            pl.BlockSpec((ka, d), lambda i: (0, 0)),
            pl.BlockSpec((kc, d), lambda i: (1, 0)),
            pl.BlockSpec((1, d), lambda i: (0, 0)),
        ],
        out_specs=[pl.BlockSpec((tm, d), lambda i: (i, 0)), pl.BlockSpec((tm, d), lambda i: (i, 0))],
        out_shape=[jax.ShapeDtypeStruct((m, d), F32), jax.ShapeDtypeStruct((m, d), BF16)],
        compiler_params=pltpu.CompilerParams(
            dimension_semantics=("arbitrary",), vmem_limit_bytes=_vmem_limit(est)),
        name="out_proj",
    )(x2d, a2d, c2d, w_out, w_out, g.reshape(1, d))


FFN_PIPE_DEPTH = 1
FFN_ROW_CHUNK = 64
FFN_LANE_CHUNK = 128
FFN_COL_CHUNK = 512


def _ffn_kernel(hf_ref, x1_ref, wua_ref, wub_ref, cwa_ref, cwb_ref, cba_ref, cbb_ref, wd_ref, gf_ref,
                o_ref, acc_ref, sa0_ref, sb0_ref, sa1_ref, sb1_ref, act_ref,
                carry_a_ref, carry_b_ref, *, tm, nf, n_items, tiles_per_seq, final_norm):
    t = pl.program_id(0)
    ta = jnp.minimum(t, n_items - 1)
    fa = ta % nf
    seq_start = ((ta // nf) % tiles_per_seq) == 0
    d = o_ref.shape[1]
    sub = V7X_SUBLANES
    lane_tiles = lambda width: [(lt, slice(lt * FFN_LANE_CHUNK, (lt + 1) * FFN_LANE_CHUNK))
                                for lt in range(width // FFN_LANE_CHUNK)]

    @pl.when(t == 0)
    def _():
        for ref in (acc_ref, sa1_ref, sb1_ref, carry_a_ref, carry_b_ref):
            ref[...] = jnp.zeros(ref.shape, ref.dtype)

    def body(s_w, s_r):
        hf = hf_ref[...]

        def up_half(wu_ref, sw_ref, carry_ref):
            up = jnp.dot(hf, wu_ref[...], preferred_element_type=F32)
            halo = jnp.where(seq_start, 0.0, carry_ref[fa])
            for lt, cols in lane_tiles(up.shape[1]):
                sw_ref[lt, 0:FFN_HALO, :] = halo[:, cols]
                sw_ref[lt, FFN_HALO:, :] = up[:, cols]
            carry_ref[fa] = up[tm - FFN_HALO:, :]

        up_half(wua_ref, s_w[0], carry_a_ref)

        def taps(cw_ref, cb_ref, cols):
            bc = lambda v: jnp.broadcast_to(v, (sub, FFN_LANE_CHUNK))
            return [bc(cw_ref[k:k + 1, cols]) for k in range(FFN_CONV_KERNEL)], bc(cb_ref[:, cols])

        def conv(sr_ref, w, bias, r0, lt):
            back = FFN_CONV_KERNEL - 1
            v = [sr_ref[lt, pl.ds(FFN_HALO + r0 + m, sub, stride=sub), :] for m in range(-back, sub)]
            return [bias + sum(v[m + k] * w[k] for k in range(FFN_CONV_KERNEL)) for m in range(sub)]

        for lt, cols in lane_tiles(act_ref.shape[1]):
            wa, ba = taps(cwa_ref, cba_ref, cols)
            wb, bb = taps(cwb_ref, cbb_ref, cols)
            for r0 in range(0, tm, FFN_ROW_CHUNK):
                ya = jnp.concatenate(conv(s_r[0], wa, ba, r0, lt), axis=0)
                yb = jnp.concatenate(conv(s_r[1], wb, bb, r0, lt), axis=0)
                act_ref[r0:r0 + FFN_ROW_CHUNK, cols] = (ya * jax.nn.sigmoid(ya) * yb).astype(BF16)

        act = act_ref[...]
        for c0 in range(0, d, FFN_COL_CHUNK):
            part = jnp.dot(act, wd_ref[:, c0:c0 + FFN_COL_CHUNK].astype(BF16), preferred_element_type=F32)
            for lt, cols in lane_tiles(FFN_COL_CHUNK):
                acc_ref[c0 // FFN_LANE_CHUNK + lt] += part[:, cols]

        up_half(wub_ref, s_w[1], carry_b_ref)

    @pl.when(t % 2 == 0)
    def _():
        body((sa0_ref, sb0_ref), (sa1_ref, sb1_ref))

    @pl.when(t % 2 == 1)
    def _():
        body((sa1_ref, sb1_ref), (sa0_ref, sb0_ref))

    @pl.when(t == 0)
    def _():
        acc_ref[...] = jnp.zeros(acc_ref.shape, F32)

    @pl.when((t >= FFN_PIPE_DEPTH) & ((t - FFN_PIPE_DEPTH) % nf == nf - 1))
    def _():
        for lt, cols in lane_tiles(d):
            for r0 in range(0, tm, FFN_ROW_CHUNK):
                for j in range(FFN_ROW_CHUNK // sub):
                    rows = slice(r0 + sub * j, r0 + sub * (j + 1))
                    o_ref[rows, cols] = x1_ref[rows, cols] + acc_ref[lt, pl.ds(r0 + j, sub, stride=sub), :]
        if final_norm:
            o_ref[...] = _rms(o_ref[...], gf_ref[...])
        acc_ref[...] = jnp.zeros(acc_ref.shape, F32)


def _conv_ffn(hf, x1, w_up, cw, cb, w_down, g_final, seq, final_norm, tm=512, tf=512):
    m, d = hf.shape
    dff = w_down.shape[0]
    nf = dff // tf
    n_items = (m // tm) * nf
    s_bytes = (tm + FFN_HALO) * tf * 4
    est = (2 * tm * d * 2 + 2 * tm * d * 4 * 2 + tm * d * 4 + 2 * 2 * d * tf * 2 + 2 * tf * d * 4
           + 4 * s_bytes + tm * tf * 2 + 2 * nf * FFN_HALO * tf * 4 + 4 * tm * tf * 4)

    item = lambda t, lag: jnp.clip(t - lag, 0, n_items - 1)
    row_a = lambda t: (item(t, 0) // nf, 0)
    row_c = lambda t: (item(t, FFN_PIPE_DEPTH) // nf, 0)
    vspec = lambda r, half: pl.BlockSpec((r, tf), lambda t: (0, half * nf + item(t, FFN_PIPE_DEPTH) % nf))
    cb2 = cb.reshape(1, 2 * dff)
    w_up_bf = w_up.astype(BF16)
    assert FFN_ROW_CHUNK == V7X_SUBLANES * V7X_SUBLANES and tm % FFN_ROW_CHUNK == 0
    s_scratch = pltpu.VMEM((tf // FFN_LANE_CHUNK, tm + FFN_HALO, FFN_LANE_CHUNK), F32)
    return pl.pallas_call(
        functools.partial(_ffn_kernel, tm=tm, nf=nf, n_items=n_items, tiles_per_seq=seq // tm,
                          final_norm=final_norm),
        grid=(n_items + FFN_PIPE_DEPTH,),
        in_specs=[
            pl.BlockSpec((tm, d), row_a),
            pl.BlockSpec((tm, d), row_c),
            pl.BlockSpec((d, tf), lambda t: (0, item(t, 0) % nf)),
            pl.BlockSpec((d, tf), lambda t: (0, nf + item(t, 0) % nf)),
            vspec(FFN_CONV_KERNEL, 0), vspec(FFN_CONV_KERNEL, 1), vspec(1, 0), vspec(1, 1),
            pl.BlockSpec((tf, d), lambda t: (item(t, FFN_PIPE_DEPTH) % nf, 0)),
            pl.BlockSpec((1, d), lambda t: (0, 0)),
        ],
        out_specs=pl.BlockSpec((tm, d), row_c),
        out_shape=jax.ShapeDtypeStruct((m, d), F32),
        scratch_shapes=[
            pltpu.VMEM((d // FFN_LANE_CHUNK, tm, FFN_LANE_CHUNK), F32),
            s_scratch, s_scratch, s_scratch, s_scratch,
            pltpu.VMEM((tm, tf), BF16),
            pltpu.VMEM((nf, FFN_HALO, tf), F32),
            pltpu.VMEM((nf, FFN_HALO, tf), F32),
        ],
        compiler_params=pltpu.CompilerParams(
            dimension_semantics=("arbitrary",), vmem_limit_bytes=_vmem_limit(est)),
        name="conv_ffn",
    )(hf, x1, w_up_bf, w_up_bf, cw, cw, cb2, cb2, w_down, g_final.reshape(1, d))


def kernel(x, norm_mix_g, w_in, attn_out_g, conv_dw_w, conv_dw_b, conv_ln_g, conv_ln_b,
           w_out, norm_ffn_g, w_up, ffn_dw_w, ffn_dw_b, w_down, norm_final_g):
    b, s, d = x.shape
    depth = w_in.shape[0]
    attn_width = attn_out_g.shape[1]
    conv_width = conv_dw_b.shape[1]
    n_heads = attn_width // HEAD_DIM
    assert s % MOBA_BLOCK == 0 and w_in.shape[2] == 3 * attn_width + 2 * conv_width

    xf = x.reshape(b * s, d)
    for l in range(depth):
        u = _in_proj(xf, norm_mix_g[l], w_in[l])
        u3 = u.reshape(b, s, -1)
        a = _moba_attn(u3, attn_out_g[l], n_heads)
        c = _conv_mod(u3, 3 * attn_width, conv_width, conv_dw_w[l], conv_dw_b[l], conv_ln_g[l], conv_ln_b[l])
        x1, hf = _out_proj(xf, a.reshape(b * s, attn_width), c.reshape(b * s, conv_width),
                           w_out[l].astype(BF16), norm_ffn_g[l])
        xf = _conv_ffn(hf, x1, w_up[l], ffn_dw_w[l], ffn_dw_b[l], w_down[l],
                       norm_final_g, s, final_norm=(l == depth - 1))
    return xf.reshape(b, s, d)
```
